```python
import math
import jax, jax.numpy as jnp
from jax import lax
import numpy as np

D_MODEL = 2048
BATCH = 16
SEQ = 256
DEPTH = 4
DEC_BATCH = 2
DEC_SEQ = 2048
PAST_LEN = 512

GRID_W = 64
N_MIXERS = 2
N_RG_LAYERS = (DEPTH + 1) // 2
N_ATTN_LAYERS = DEPTH // 2
N_HEADS = 16
N_KV_HEADS = 4
HEAD_DIM = D_MODEL // N_HEADS
ROT_PAIRS = HEAD_DIM // 4
ROPE_THETA = 10000.0
Q_BLOCK = 128
RNN_WIDTH = D_MODEL
RG_BLOCKS = 16
RG_BLOCK_SIZE = RNN_WIDTH // RG_BLOCKS
RG_C = 8.0
CONV_W = 4
CONV_LEFT = 1
D_FF = 4 * D_MODEL
EPS = 1e-6

kernel_name = "hybrid_rglru_gqa_diffusion_step"


def rmsnorm(x, g):
    xf = x.astype(jnp.float32)
    y = xf * lax.rsqrt(jnp.mean(xf * xf, axis=-1, keepdims=True) + EPS)
    return (y * g.astype(jnp.float32)).astype(x.dtype)


def modulation(cvec, w_mod_l, b_mod_l):
    m = jax.nn.silu(cvec) @ w_mod_l + b_mod_l
    return [t[:, None, :] for t in jnp.split(m, 6, axis=-1)]


def depthwise_conv(x, w, b):
    T = x.shape[1]
    xp = jnp.pad(x, ((0, 0), (CONV_LEFT, CONV_W - 1 - CONV_LEFT), (0, 0)))
    y = b + xp[:, 0:T] * w[0]
    for k in range(1, CONV_W):
        y = y + xp[:, k:k + T] * w[k]
    return y


def rglru_mixer(h, h0, w_in, conv_w, conv_b, w_a, b_a, w_x, b_x, lam, w_out):
    B, T, _ = h.shape
    proj = h @ w_in
    gate_branch, xb = jnp.split(proj, 2, axis=-1)
    xb = depthwise_conv(xb, conv_w, conv_b)
    xblk = xb.reshape(B, T, RG_BLOCKS, RG_BLOCK_SIZE)
    r = jax.nn.sigmoid(jnp.einsum('btnj,enjk->betnk', xblk, w_a).reshape(B, 2, T, RNN_WIDTH)
                       + b_a[None, :, None, :]).astype(jnp.float32)
    i = jax.nn.sigmoid(jnp.einsum('btnj,enjk->betnk', xblk, w_x).reshape(B, 2, T, RNN_WIDTH)
                       + b_x[None, :, None, :]).astype(jnp.float32)
    log_a = -RG_C * jax.nn.softplus(-lam.astype(jnp.float32))[None, :, None, :] * r
    a = jnp.exp(log_a)
    u = jnp.sqrt(-jnp.expm1(2.0 * log_a)) * i * xb.astype(jnp.float32)[:, None]
    a = jnp.stack([a[:, 0], jnp.flip(a[:, 1], axis=1)], axis=1)
    u = jnp.stack([u[:, 0], jnp.flip(u[:, 1], axis=1)], axis=1)
    a_t = jnp.moveaxis(a, 2, 0)
    u_t = jnp.moveaxis(u, 2, 0)

    def step(hc, inp):
        at, ut = inp
        hn = at * hc + ut
        return hn, hn

    h_final, hs = lax.scan(step, h0.astype(jnp.float32), (a_t, u_t))
    y = hs[:, :, 0] + jnp.flip(hs[:, :, 1], axis=0)
    y = jnp.moveaxis(y, 0, 1).astype(h.dtype)
    out = (y * jax.nn.gelu(gate_branch)) @ w_out
    return out, h_final.astype(h.dtype)


def qkv_heads(h, w_qkv, g_q, g_k):
    B, T, _ = h.shape
    qkv = h @ w_qkv
    q = qkv[..., :N_HEADS * HEAD_DIM].reshape(B, T, N_HEADS, HEAD_DIM)
    k = qkv[..., N_HEADS * HEAD_DIM:(N_HEADS + N_KV_HEADS) * HEAD_DIM].reshape(B, T, N_KV_HEADS, HEAD_DIM)
    v = qkv[..., (N_HEADS + N_KV_HEADS) * HEAD_DIM:].reshape(B, T, N_KV_HEADS, HEAD_DIM)
    return rmsnorm(q, g_q), rmsnorm(k, g_k), v


def grid_rope_tables(n_tokens):
    rows = n_tokens // GRID_W
    row = jnp.broadcast_to(jnp.arange(rows, dtype=jnp.float32)[:, None], (rows, GRID_W)).reshape(-1)
    col = jnp.broadcast_to(jnp.arange(GRID_W, dtype=jnp.float32)[None, :], (rows, GRID_W)).reshape(-1)
    inv_freq = ROPE_THETA ** (-jnp.arange(ROT_PAIRS, dtype=jnp.float32) / ROT_PAIRS)
    ang_r = (row[:, None] * inv_freq[None, :])[:, None, :]
    ang_c = (col[:, None] * inv_freq[None, :])[:, None, :]
    return jnp.cos(ang_r), jnp.sin(ang_r), jnp.cos(ang_c), jnp.sin(ang_c)


def apply_rope2d(x, tables):
    cos_r, sin_r, cos_c, sin_c = tables
    xf = x.astype(jnp.float32)
    xr, xc = jnp.split(xf, 2, axis=-1)

    def rot(z, cos, sin):
        z1, z2 = jnp.split(z, 2, axis=-1)
        return jnp.concatenate([z1 * cos - z2 * sin, z2 * cos + z1 * sin], axis=-1)

    return jnp.concatenate([rot(xr, cos_r, sin_r), rot(xc, cos_c, sin_c)], axis=-1).astype(x.dtype)


def blocked_attention(q, k, v):
    B, S, H, hd = q.shape
    G = H // N_KV_HEADS
    nb = S // Q_BLOCK
    qb = q.reshape(B, nb, Q_BLOCK, N_KV_HEADS, G, hd).transpose(1, 0, 2, 3, 4, 5)
    scale = 1.0 / math.sqrt(hd)

    def one_block(qblk):
        s = jnp.einsum('bqkgd,btkd->bkgqt', qblk, k).astype(jnp.float32) * scale
        p = jax.nn.softmax(s, axis=-1).astype(v.dtype)
        return jnp.einsum('bkgqt,btkd->bqkgd', p, v)

    o = lax.map(one_block, qb)
    return o.transpose(1, 0, 2, 3, 4, 5).reshape(B, S, H * hd)


def squared_relu_mlp(x, w1, w2):
    hdn = jax.nn.relu(x @ w1)
    return (hdn * hdn) @ w2


def setup_inputs(seed: int = 0) -> dict:
    key = jax.random.key(seed)
    ks = iter(jax.random.split(key, 40))

    def nrm(shape, scale):
        return jax.random.normal(next(ks), shape, jnp.float32) * scale

    def gain(shape):
        return 1.0 + nrm(shape, 0.02)

    D = D_MODEL
    qkv_w = (N_HEADS + 2 * N_KV_HEADS) * HEAD_DIM
    a0 = jax.random.uniform(next(ks), (N_RG_LAYERS, 2, RNN_WIDTH), jnp.float32, 0.9, 0.999)
    p = a0 ** (1.0 / RG_C)
    rg_lambda = jnp.log(p) - jnp.log1p(-p)
    return {
        "x_prompt": nrm((BATCH, SEQ, D), 1.0),
        "x_sample": nrm((DEC_BATCH, DEC_SEQ, D), 1.0),
        "state_rglru": nrm((DEC_BATCH, N_RG_LAYERS, 2, RNN_WIDTH), 0.5),
        "cache_k": nrm((DEC_BATCH, N_ATTN_LAYERS, PAST_LEN, N_KV_HEADS, HEAD_DIM), 1.0),
        "cache_v": nrm((DEC_BATCH, N_ATTN_LAYERS, PAST_LEN, N_KV_HEADS, HEAD_DIM), 1.0),
        "c": nrm((DEC_BATCH, D), 1.0),
        "c_ctx": nrm((D,), 1.0),
        "w_mod": nrm((DEPTH, D, 6 * D), 0.5 * D ** -0.5),
        "b_mod": nrm((DEPTH, 6 * D), 0.01),
        "g_pre_mix": gain((DEPTH, D)),
        "g_post_mix": gain((DEPTH, D)),
        "g_pre_ffn": gain((DEPTH, D)),
        "g_post_ffn": gain((DEPTH, D)),
        "w_qkv": nrm((N_ATTN_LAYERS, D, qkv_w), D ** -0.5),
        "g_q": gain((N_ATTN_LAYERS, HEAD_DIM)),
        "g_k": gain((N_ATTN_LAYERS, HEAD_DIM)),
        "w_o": nrm((N_ATTN_LAYERS, N_HEADS * HEAD_DIM, D), (N_HEADS * HEAD_DIM) ** -0.5),
        "w_rg_in": nrm((N_RG_LAYERS, D, 2 * RNN_WIDTH), D ** -0.5),
        "rg_conv_w": nrm((N_RG_LAYERS, CONV_W, RNN_WIDTH), CONV_W ** -0.5),
        "rg_conv_b": nrm((N_RG_LAYERS, RNN_WIDTH), 0.01),
        "w_rg_a": nrm((N_RG_LAYERS, 2, RG_BLOCKS, RG_BLOCK_SIZE, RG_BLOCK_SIZE), RG_BLOCK_SIZE ** -0.5),
        "b_rg_a": nrm((N_RG_LAYERS, 2, RNN_WIDTH), 0.01),
        "w_rg_x": nrm((N_RG_LAYERS, 2, RG_BLOCKS, RG_BLOCK_SIZE, RG_BLOCK_SIZE), RG_BLOCK_SIZE ** -0.5),
        "b_rg_x": nrm((N_RG_LAYERS, 2, RNN_WIDTH), 0.01),
        "rg_lambda": rg_lambda,
        "w_rg_out": nrm((N_RG_LAYERS, RNN_WIDTH, D), RNN_WIDTH ** -0.5),
        "w_ff1": nrm((DEPTH, D, D_FF), D ** -0.5),
        "w_ff2": nrm((DEPTH, D_FF, D), D_FF ** -0.5),
    }


def reference(x_prompt, x_sample, state_rglru, cache_k, cache_v, c, c_ctx,
              w_mod, b_mod, g_pre_mix, g_post_mix, g_pre_ffn, g_post_ffn,
              w_qkv, g_q, g_k, w_o,
              w_rg_in, rg_conv_w, rg_conv_b, w_rg_a, b_rg_a, w_rg_x, b_rg_x, rg_lambda, w_rg_out,
              w_ff1, w_ff2):
    xp = x_prompt
    xs = x_sample
    rope_tab = grid_rope_tables(xs.shape[1])
    new_states, new_ks, new_vs = [], [], []

    for l in range(DEPTH):
        mp = modulation(c_ctx[None, :], w_mod[l], b_mod[l])
        ms = modulation(c, w_mod[l], b_mod[l])
        hp = rmsnorm(xp, g_pre_mix[l]) * (1.0 + mp[1]) + mp[0]
        hs = rmsnorm(xs, g_pre_mix[l]) * (1.0 + ms[1]) + ms[0]
        j = l // N_MIXERS
        if l % N_MIXERS == 0:
            rg = (w_rg_in[j], rg_conv_w[j], rg_conv_b[j], w_rg_a[j], b_rg_a[j],
                  w_rg_x[j], b_rg_x[j], rg_lambda[j], w_rg_out[j])
            h0 = jnp.zeros((xp.shape[0], 2, RNN_WIDTH), xp.dtype)
            op, st = rglru_mixer(hp, h0, *rg)
            os_, _ = rglru_mixer(hs, state_rglru[:, j], *rg)
            new_states.append(st)
        else:
            qp, kp, vp = qkv_heads(hp, w_qkv[j], g_q[j], g_k[j])
            op = blocked_attention(qp, kp, vp) @ w_o[j]
            new_ks.append(kp)
            new_vs.append(vp)
            qs, ks_, vs_ = qkv_heads(hs, w_qkv[j], g_q[j], g_k[j])
            qs = apply_rope2d(qs, rope_tab)
            ks_ = apply_rope2d(ks_, rope_tab)
            k_all = jnp.concatenate([ks_, cache_k[:, j]], axis=1)
            v_all = jnp.concatenate([vs_, cache_v[:, j]], axis=1)
            os_ = blocked_attention(qs, k_all, v_all) @ w_o[j]
        xp = xp + mp[2] * rmsnorm(op, g_post_mix[l])
        xs = xs + ms[2] * rmsnorm(os_, g_post_mix[l])
        fp = rmsnorm(xp, g_pre_ffn[l]) * (1.0 + mp[4]) + mp[3]
        fs = rmsnorm(xs, g_pre_ffn[l]) * (1.0 + ms[4]) + ms[3]
        xp = xp + mp[5] * rmsnorm(squared_relu_mlp(fp, w_ff1[l], w_ff2[l]), g_post_ffn[l])
        xs = xs + ms[5] * rmsnorm(squared_relu_mlp(fs, w_ff1[l], w_ff2[l]), g_post_ffn[l])

    new_state_rglru = jnp.stack(new_states, axis=1)
    new_cache_k = jnp.stack(new_ks, axis=1)
    new_cache_v = jnp.stack(new_vs, axis=1)
    return (xp, xs, new_state_rglru, new_cache_k, new_cache_v)
```

```python
import functools
import math
from typing import NamedTuple

import jax
import jax.numpy as jnp
from jax import lax
from jax.experimental import pallas as pl
from jax.experimental.pallas import tpu as pltpu

_BF = jnp.bfloat16
_F32 = jnp.float32

_EPS = 1e-6
_RG_C = 8.0
_ROPE_THETA = 10000.0
_CONV_W = 4
_LANES = 128
_SUBLANES = 8
_MIB = 1024 * 1024


class _Cfg(NamedTuple):
    d_model: int
    batch: int
    seq: int
    depth: int
    dec_batch: int
    dec_seq: int
    past_len: int
    grid_w: int
    n_heads: int
    n_kv_heads: int
    rg_blocks: int
    d_ff: int
    dec_chunks: int
    tm: int
    tn_in: int
    tn_qkv: int
    tn_ff: int
    tk: int
    tn_mod: int
    tq: int
    rg_rows: int
    epi_rows: int


_CFG = _Cfg(d_model=2048, batch=16, seq=256, depth=4, dec_batch=2, dec_seq=2048, past_len=512,
            grid_w=64, n_heads=16, n_kv_heads=4, rg_blocks=16, d_ff=8192, dec_chunks=4,
            tm=512, tn_in=1024, tn_qkv=512, tn_ff=1024, tk=512, tn_mod=1024, tq=256,
            rg_rows=256, epi_rows=64)


def _params(sem, vmem_mib):
    return pltpu.CompilerParams(dimension_semantics=sem, vmem_limit_bytes=vmem_mib * _MIB)


def _rms(x, g):
    return x * lax.rsqrt(jnp.mean(x * x, axis=-1, keepdims=True) + _EPS) * g


def _mod_row(cfg, tm):
    npb = cfg.batch * cfg.seq // tm
    bps = cfg.dec_seq // tm
    return lambda i: jnp.where(i < npb, 0, 1 + (i - npb) // bps)


def _mod_kernel(c_ref, w_ref, b_ref, o_ref):
    s = jax.nn.silu(c_ref[...]).astype(_BF)
    o_ref[...] = jnp.dot(s, w_ref[...].astype(_BF), preferred_element_type=_F32) + b_ref[...]


def _modulation(cfg, cvec, w_mod, b_mod):
    L, D, N = w_mod.shape
    tn = cfg.tn_mod
    return pl.pallas_call(
        _mod_kernel,
        grid=(L, N // tn),
        in_specs=[pl.BlockSpec((_SUBLANES, D), lambda l, j: (0, 0)),
                  pl.BlockSpec((None, D, tn), lambda l, j: (l, 0, j)),
                  pl.BlockSpec((None, 1, tn), lambda l, j: (l, 0, j))],
        out_specs=pl.BlockSpec((None, _SUBLANES, tn), lambda l, j: (l, 0, j)),
        out_shape=jax.ShapeDtypeStruct((L, _SUBLANES, N), _F32),
        compiler_params=_params(("arbitrary", "arbitrary"), 40),
        name="modulation",
    )(cvec, w_mod, b_mod.reshape(L, 1, N))


def _prenorm_kernel(x_ref, g_ref, sh_ref, sc_ref, o_ref):
    o_ref[...] = (_rms(x_ref[...], g_ref[...]) * (1.0 + sc_ref[...]) + sh_ref[...]).astype(_BF)


def _prenorm(cfg, x, g, mod, l):
    M, D = x.shape
    tm = cfg.tm
    row = _mod_row(cfg, tm)
    mspec = lambda idx: pl.BlockSpec((None, None, None, 1, D), lambda i: (l, row(i), idx, 0, 0))
    return pl.pallas_call(
        _prenorm_kernel,
        grid=(M // tm,),
        in_specs=[pl.BlockSpec((tm, D), lambda i: (i, 0)),
                  pl.BlockSpec((None, 1, D), lambda i: (l, 0, 0)),
                  mspec(0), mspec(1)],
        out_specs=pl.BlockSpec((tm, D), lambda i: (i, 0)),
        out_shape=jax.ShapeDtypeStruct((M, D), _BF),
        compiler_params=_params(("arbitrary",), 32),
        name="prenorm",
    )(x, g, mod, mod)


def _wres_dot(a_ref, w_ref, wbf_ref):
    @pl.when(pl.program_id(1) == 0)
    def _cast():
        wbf_ref[...] = w_ref[...].astype(_BF)
    return jnp.dot(a_ref[...], wbf_ref[...], preferred_element_type=_F32)


def _inproj_kernel(a_ref, w_ref, o_ref, wbf_ref):
    o_ref[...] = _wres_dot(a_ref, w_ref, wbf_ref)


def _ffn1_kernel(a_ref, w_ref, o_ref, wbf_ref):
    z = jnp.maximum(_wres_dot(a_ref, w_ref, wbf_ref), 0.0)
    o_ref[...] = (z * z).astype(_BF)


def _wres_call(kernel, cfg, a, w, l, tn, out_dtype, name, extra_in=(), extra_specs=()):
    M, K = a.shape
    N = w.shape[-1]
    tm = cfg.tm
    return pl.pallas_call(
        kernel,
        grid=(N // tn, M // tm),
        in_specs=[pl.BlockSpec((tm, K), lambda j, i: (i, 0)),
                  pl.BlockSpec((None, K, tn), lambda j, i: (l, 0, j)),
                  *extra_specs],
        out_specs=pl.BlockSpec((tm, tn), lambda j, i: (i, j)),
        out_shape=jax.ShapeDtypeStruct((M, N), out_dtype),
        scratch_shapes=[pltpu.VMEM((K, tn), _BF)],
        compiler_params=_params(("arbitrary", "arbitrary"), 48),
        name=name,
    )(a, w, *extra_in)


def _rope(z, cos, sin, first_half):
    hd = z.shape[-1]
    quarter = hd // 4
    partner = jnp.where(first_half, pltpu.roll(z, hd - quarter, 1), pltpu.roll(z, quarter, 1))
    return z * cos + partner * sin


def _qkv_kernel(a_ref, w_ref, gq_ref, gk_ref, cos_ref, sin_ref, o_ref, wbf_ref, *,
                n_q_tiles, n_prompt_blocks, hd):
    j = pl.program_id(0)
    i = pl.program_id(1)
    acc = _wres_dot(a_ref, w_ref, wbf_ref)
    heads = acc.shape[-1] // hd
    normed = j <= n_q_tiles
    g = jnp.where(j < n_q_tiles, gq_ref[...], gk_ref[...])

    @pl.when(jnp.logical_not(normed))
    def _v():
        o_ref[...] = acc

    @pl.when(jnp.logical_and(normed, i < n_prompt_blocks))
    def _context():
        for h in range(heads):
            o_ref[:, h * hd:(h + 1) * hd] = _rms(acc[:, h * hd:(h + 1) * hd], g)

    @pl.when(jnp.logical_and(normed, i >= n_prompt_blocks))
    def _latent():
        cos = cos_ref[...]
        sin = sin_ref[...]
        lane = lax.broadcasted_iota(jnp.int32, cos.shape, 1)
        first_half = (lane % (hd // 2)) < (hd // 4)
        for h in range(heads):
            z = _rms(acc[:, h * hd:(h + 1) * hd], g)
            o_ref[:, h * hd:(h + 1) * hd] = _rope(z, cos, sin, first_half)


def _rope_tables(cfg):
    hd = cfg.d_model // cfg.n_heads
    pairs = hd // 4
    n = cfg.dec_seq
    rows = n // cfg.grid_w
    row = jnp.broadcast_to(jnp.arange(rows, dtype=_F32)[:, None], (rows, cfg.grid_w)).reshape(-1)
    col = jnp.broadcast_to(jnp.arange(cfg.grid_w, dtype=_F32)[None, :], (rows, cfg.grid_w)).reshape(-1)
    inv_freq = _ROPE_THETA ** (-jnp.arange(pairs, dtype=_F32) / pairs)
    ang_r = row[:, None] * inv_freq[None, :]
    ang_c = col[:, None] * inv_freq[None, :]
    cos = jnp.concatenate([jnp.cos(ang_r), jnp.cos(ang_r), jnp.cos(ang_c), jnp.cos(ang_c)], axis=-1)
    sin = jnp.concatenate([-jnp.sin(ang_r), jnp.sin(ang_r), -jnp.sin(ang_c), jnp.sin(ang_c)], axis=-1)
    return cos, sin


def _qkv(cfg, h, w_qkv, g_q, g_k, rope, j):
    hd = cfg.d_model // cfg.n_heads
    tm, tn = cfg.tm, cfg.tn_qkv
    npb = cfg.batch * cfg.seq // tm
    bps = cfg.dec_seq // tm
    assert (cfg.n_kv_heads * hd) == tn, "one column tile must hold exactly the k heads"
    tab = pl.BlockSpec((tm, hd), lambda jj, i: (jnp.maximum(i - npb, 0) % bps, 0))
    gain = pl.BlockSpec((None, 1, hd), lambda jj, i: (j, 0, 0))
    kern = functools.partial(_qkv_kernel, n_q_tiles=cfg.n_heads * hd // tn, n_prompt_blocks=npb, hd=hd)
    return _wres_call(kern, cfg, h, w_qkv, j, tn, _F32, "qkv_proj",
                      extra_in=(g_q, g_k, *rope), extra_specs=(gain, gain, tab, tab))


def _epi_kernel(*refs, nk, with_next, rows):
    if with_next:
        a_ref, w_ref, x_ref, gate_ref, gpost_ref, gnext_ref, sh_ref, sc_ref, xo_ref, ho_ref, acc_ref = refs
    else:
        a_ref, w_ref, x_ref, gate_ref, gpost_ref, xo_ref, acc_ref = refs
    k = pl.program_id(1)
    part = jnp.dot(a_ref[...], w_ref[...].astype(_BF), preferred_element_type=_F32)

    @pl.when(k == 0)
    def _first():
        acc_ref[...] = part

    @pl.when(k > 0)
    def _rest():
        acc_ref[...] += part

    @pl.when(k == nk - 1)
    def _epilogue():
        gate = gate_ref[...]
        gpost = gpost_ref[...]

        def chunk(c, carry):
            r = pl.ds(pl.multiple_of(c * rows, rows), rows)
            xn = x_ref[r, :] + gate * _rms(acc_ref[r, :], gpost)
            xo_ref[r, :] = xn
            if with_next:
                ho_ref[r, :] = (_rms(xn, gnext_ref[...]) * (1.0 + sc_ref[...]) + sh_ref[...]).astype(_BF)
            return carry

        lax.fori_loop(0, acc_ref.shape[0] // rows, chunk, 0)


def _epi(cfg, a, w, wl, x, mod, l, gate_idx, gpost, nxt, name):
    M, K = a.shape
    D = w.shape[-1]
    tm, tk = cfg.tm, cfg.tk
    nk = K // tk
    row = _mod_row(cfg, tm)
    mspec = lambda ml, idx: pl.BlockSpec((None, None, None, 1, D), lambda i, k: (ml, row(i), idx, 0, 0))
    gspec = lambda gl: pl.BlockSpec((None, 1, D), lambda i, k: (gl, 0, 0))
    xspec = pl.BlockSpec((tm, D), lambda i, k: (i, 0))
    in_specs = [pl.BlockSpec((tm, tk), lambda i, k: (i, k)),
                pl.BlockSpec((None, tk, D), lambda i, k: (wl, k, 0)),
                xspec, mspec(l, gate_idx), gspec(l)]
    args = [a, w, x, mod, gpost]
    out_specs = [xspec]
    out_shape = [jax.ShapeDtypeStruct((M, D), _F32)]
    if nxt is not None:
        gnext, gl, ml, sh_idx, sc_idx = nxt
        in_specs += [gspec(gl), mspec(ml, sh_idx), mspec(ml, sc_idx)]
        args += [gnext, mod, mod]
        out_specs.append(xspec)
        out_shape.append(jax.ShapeDtypeStruct((M, D), _BF))
    out = pl.pallas_call(
        functools.partial(_epi_kernel, nk=nk, with_next=nxt is not None, rows=cfg.epi_rows),
        grid=(M // tm, nk),
        in_specs=in_specs,
        out_specs=out_specs,
        out_shape=out_shape,
        scratch_shapes=[pltpu.VMEM((tm, D), _F32)],
        compiler_params=_params(("arbitrary", "arbitrary"), 48),
        name=name,
    )(*args)
    return out if nxt is not None else (out[0], None)


def _rg_kernel(*refs, S, Tc, nchunk, RB, want_final):
    (xb_ref, gt_ref, cw_ref, cb_ref, wa_ref, ba_ref, wx_ref, bx_ref, lam_ref, h0_ref), rest = refs[:10], refs[10:]
    if want_final:
        yg_ref, hfin_ref, xp_s, a_s, u_s, cin_s = rest
    else:
        yg_ref, xp_s, a_s, u_s, cin_s = rest
    R = S * Tc
    T = Tc * nchunk
    C = xb_ref.shape[-1]
    nlt = C // _LANES
    pad = _SUBLANES

    xp_s[0:pad, :] = jnp.zeros((pad, C), _F32)
    xp_s[pad + R:2 * pad + R, :] = jnp.zeros((pad, C), _F32)
    xp_s[pad:pad + R, :] = xb_ref[...]

    wa = wa_ref[...].astype(_BF)
    wx = wx_ref[...].astype(_BF)
    ce = -_RG_C * jax.nn.softplus(-lam_ref[...])
    cw = cw_ref[...]
    cb = cb_ref[...]

    def block_diag(x16, w, e):
        return jnp.concatenate(
            [jnp.dot(x16[:, n * _LANES:(n + 1) * _LANES], w[e, n], preferred_element_type=_F32)
             for n in range(nlt)], axis=-1) if nlt > 1 else jnp.dot(x16, w[e, 0], preferred_element_type=_F32)

    def gates(c, carry):
        r0 = pl.multiple_of(c * RB, RB)
        win = xp_s[pl.ds(r0, RB + 2 * pad), :]
        n = RB + 2 * pad
        tt = (r0 + lax.broadcasted_iota(jnp.int32, (RB, C), 0)) & (T - 1)
        x0 = win[pad:pad + RB]
        xm1 = jnp.where(tt >= 1, pltpu.roll(win, 1, 0)[pad:pad + RB], 0.0)
        xp1 = jnp.where(tt < T - 1, pltpu.roll(win, n - 1, 0)[pad:pad + RB], 0.0)
        xp2 = jnp.where(tt < T - 2, pltpu.roll(win, n - 2, 0)[pad:pad + RB], 0.0)
        xc = cb + xm1 * cw[0:1] + x0 * cw[1:2] + xp1 * cw[2:3] + xp2 * cw[3:4]
        x16 = xc.astype(_BF)
        s = r0 // Tc
        t0 = r0 % Tc
        dst = pl.ds(t0 * S + s, RB, stride=S)
        for e in range(2):
            r = jax.nn.sigmoid(block_diag(x16, wa, e) + ba_ref[e])
            i = jax.nn.sigmoid(block_diag(x16, wx, e) + bx_ref[e])
            log_a = ce[e] * r
            a = jnp.exp(log_a)
            a_s[e, dst, :] = a
            u_s[e, dst, :] = jnp.sqrt(-jnp.tanh(log_a) * (a * a + 1.0)) * i * xc
        return carry

    lax.fori_loop(0, R // RB, gates, 0)

    chunked = nchunk > 1

    def step(t, carry):
        hf, hb, pf, pb = carry
        rf = pl.ds(pl.multiple_of(t * S, S), S)
        rb = pl.ds(pl.multiple_of((Tc - 1 - t) * S, S), S)
        af = a_s[0, rf, :]
        ab = a_s[1, rb, :]
        hf = af * hf + u_s[0, rf, :]
        hb = ab * hb + u_s[1, rb, :]
        u_s[0, rf, :] = hf
        u_s[1, rb, :] = hb
        if chunked:
            pf = af * pf
            pb = ab * pb
            a_s[0, rf, :] = pf
            a_s[1, rb, :] = pb
        return hf, hb, pf, pb

    zeros = jnp.zeros((S, C), _F32)
    ones = jnp.ones((S, C), _F32)
    init = (zeros, zeros, ones, ones) if chunked else (h0_ref[0], h0_ref[1], ones, ones)
    hf, hb, pf, pb = lax.fori_loop(0, Tc, step, init)

    if want_final:
        hfin_ref[0] = hf
        hfin_ref[1] = hb

    if chunked:
        chunk_id = lax.broadcasted_iota(jnp.int32, (S, C), 0) % nchunk
        first = chunk_id == 0
        last = chunk_id == nchunk - 1
        cin_f = jnp.where(first, h0_ref[0], 0.0)
        cin_b = jnp.where(last, h0_ref[1], 0.0)
        for _ in range(nchunk - 1):
            cin_f = jnp.where(first, h0_ref[0], pltpu.roll(hf + pf * cin_f, 1, 0))
            cin_b = jnp.where(last, h0_ref[1], pltpu.roll(hb + pb * cin_b, S - 1, 0))
        cin_s[0] = cin_f
        cin_s[1] = cin_b

    def emit(c, carry):
        r0 = pl.multiple_of(c * RB, RB)
        s = r0 // Tc
        t0 = r0 % Tc
        src = pl.ds(t0 * S + s, RB, stride=S)
        yf = u_s[0, src, :]
        yb = u_s[1, src, :]
        if chunked:
            yf = yf + a_s[0, src, :] * cin_s[0, pl.ds(s, 1), :]
            yb = yb + a_s[1, src, :] * cin_s[1, pl.ds(s, 1), :]
        r = pl.ds(r0, RB)
        yg_ref[r, :] = ((yf + yb) * jax.nn.gelu(gt_ref[r, :])).astype(_BF)
        return carry

    lax.fori_loop(0, R // RB, emit, 0)


def _rg_core(cfg, proj, row_off, S, Tc, nchunk, h0, rgw, j, want_final):
    conv_w, conv_b, w_a, b_a, w_x, b_x, lam = rgw
    rnn = proj.shape[1] // 2
    C = _LANES
    nct = rnn // C
    R = S * Tc
    RB = min(cfg.rg_rows, Tc)
    rb0 = row_off // R
    nrg = conv_w.shape[0]
    vec = lambda arr: arr.reshape(nrg, 2, 1, rnn)
    vspec = pl.BlockSpec((None, 2, 1, C), lambda ct: (j, 0, 0, ct))
    wspec = pl.BlockSpec((None, 2, 1, C, C), lambda ct: (j, 0, ct, 0, 0))
    out_specs = [pl.BlockSpec((R, C), lambda ct: (0, ct))]
    out_shape = [jax.ShapeDtypeStruct((R, rnn), _BF)]
    if want_final:
        out_specs.append(pl.BlockSpec((2, S, C), lambda ct: (0, 0, ct)))
        out_shape.append(jax.ShapeDtypeStruct((2, S, rnn), _F32))
    return pl.pallas_call(
        functools.partial(_rg_kernel, S=S, Tc=Tc, nchunk=nchunk, RB=RB, want_final=want_final),
        grid=(nct,),
        in_specs=[pl.BlockSpec((R, C), lambda ct: (rb0, nct + ct)),
                  pl.BlockSpec((R, C), lambda ct: (rb0, ct)),
                  pl.BlockSpec((None, _CONV_W, C), lambda ct: (j, 0, ct)),
                  pl.BlockSpec((None, 1, C), lambda ct: (j, 0, ct)),
                  wspec, vspec, wspec, vspec, vspec,
                  pl.BlockSpec((2, S, C), lambda ct: (0, 0, ct))],
        out_specs=out_specs,
        out_shape=out_shape,
        scratch_shapes=[pltpu.VMEM((R + 2 * _SUBLANES, C), _F32),
                        pltpu.VMEM((2, R, C), _F32),
                        pltpu.VMEM((2, R, C), _F32),
                        pltpu.VMEM((2, S, C), _F32)],
        compiler_params=_params(("arbitrary",), 40),
        name="rg_core_ctx" if want_final else "rg_core_lat",
    )(proj, proj, conv_w, conv_b.reshape(nrg, 1, rnn), w_a, vec(b_a), w_x, vec(b_x), vec(lam), h0)


def _attn_kernel(*refs, G, hd, scale, has_cache):
    if has_cache:
        q_ref, k_ref, v_ref, ck_ref, cv_ref, o_ref = refs
        kc = ck_ref[...].astype(_BF)
        vc = cv_ref[...].astype(_BF)
    else:
        q_ref, k_ref, v_ref, o_ref = refs
    k = k_ref[...].astype(_BF)
    v = v_ref[...].astype(_BF)
    nt = (((1,), (1,)), ((), ()))
    for g in range(G):
        q = q_ref[:, g * hd:(g + 1) * hd].astype(_BF)
        s1 = lax.dot_general(q, k, nt, preferred_element_type=_F32) * scale
        m = jnp.max(s1, axis=-1, keepdims=True)
        if has_cache:
            s2 = lax.dot_general(q, kc, nt, preferred_element_type=_F32) * scale
            m = jnp.maximum(m, jnp.max(s2, axis=-1, keepdims=True))
        p1 = jnp.exp(s1 - m)
        l = jnp.sum(p1, axis=-1, keepdims=True)
        o = jnp.dot(p1.astype(_BF), v, preferred_element_type=_F32)
        if has_cache:
            p2 = jnp.exp(s2 - m)
            l = l + jnp.sum(p2, axis=-1, keepdims=True)
            o = o + jnp.dot(p2.astype(_BF), vc, preferred_element_type=_F32)
        o_ref[:, g * hd:(g + 1) * hd] = (o / l).astype(_BF)


def _attention(cfg, qkv, row_off, B, T, tq, cache, j, name):
    hd = cfg.d_model // cfg.n_heads
    KV = cfg.n_kv_heads
    G = cfg.n_heads // KV
    nq = T // tq
    qb0 = row_off // tq
    kb0 = row_off // T
    kcol = cfg.n_heads
    vcol = cfg.n_heads + KV
    in_specs = [pl.BlockSpec((tq, G * hd), lambda b, h, qi: (qb0 + b * nq + qi, h)),
                pl.BlockSpec((T, hd), lambda b, h, qi: (kb0 + b, kcol + h)),
                pl.BlockSpec((T, hd), lambda b, h, qi: (kb0 + b, vcol + h))]
    args = [qkv, qkv, qkv]
    if cache is not None:
        ck, cv = cache
        P = ck.shape[2]
        cspec = pl.BlockSpec((None, None, P, hd), lambda b, h, qi: (b, j, 0, h))
        in_specs += [cspec, cspec]
        args += [ck, cv]
    return pl.pallas_call(
        functools.partial(_attn_kernel, G=G, hd=hd, scale=1.0 / math.sqrt(hd), has_cache=cache is not None),
        grid=(B, KV, nq),
        in_specs=in_specs,
        out_specs=pl.BlockSpec((tq, G * hd), lambda b, h, qi: (b * nq + qi, h)),
        out_shape=jax.ShapeDtypeStruct((B * T, cfg.n_heads * hd), _BF),
        compiler_params=_params(("arbitrary", "arbitrary", "arbitrary"), 48),
        name=name,
    )(*args)


def _forward(cfg, x_prompt, x_sample, state_rglru, cache_k, cache_v, c, c_ctx,
             w_mod, b_mod, g_pre_mix, g_post_mix, g_pre_ffn, g_post_ffn,
             w_qkv, g_q, g_k, w_o,
             w_rg_in, rg_conv_w, rg_conv_b, w_rg_a, b_rg_a, w_rg_x, b_rg_x, rg_lambda, w_rg_out,
             w_ff1, w_ff2):
    D = cfg.d_model
    L = cfg.depth
    B, T = cfg.batch, cfg.seq
    DB, DT = cfg.dec_batch, cfg.dec_seq
    NP = B * T
    hd = D // cfg.n_heads
    KV = cfg.n_kv_heads
    nch = cfg.dec_chunks

    x = jnp.concatenate([x_prompt.reshape(NP, D), x_sample.reshape(DB * DT, D)], axis=0)

    cvec = jnp.zeros((_SUBLANES, D), _F32).at[0].set(c_ctx).at[1:1 + DB].set(c)
    mod = _modulation(cfg, cvec, w_mod, b_mod)[:, :1 + DB].reshape(L, 1 + DB, 6, 1, D)

    gains = lambda g: g.reshape(g.shape[0], 1, g.shape[1])
    g_pre_mix, g_post_mix, g_pre_ffn, g_post_ffn = map(gains, (g_pre_mix, g_post_mix, g_pre_ffn, g_post_ffn))
    g_q, g_k = gains(g_q), gains(g_k)
    rope = _rope_tables(cfg)
    ck = cache_k.reshape(DB, -1, cfg.past_len, KV * hd)
    cv = cache_v.reshape(DB, -1, cfg.past_len, KV * hd)

    h = _prenorm(cfg, x, g_pre_mix, mod, 0)
    new_states, new_ks, new_vs = [], [], []
    for l in range(L):
        j = l // 2
        if l % 2 == 0:
            proj = _wres_call(_inproj_kernel, cfg, h, w_rg_in, j, cfg.tn_in, _F32, "rg_in_proj")
            rgw = (rg_conv_w, rg_conv_b, w_rg_a, b_rg_a, w_rg_x, b_rg_x, rg_lambda)
            h0p = jnp.zeros((2, B, D), _F32)
            yp, st = _rg_core(cfg, proj, 0, B, T, 1, h0p, rgw, j, True)
            h0s = jnp.repeat(jnp.swapaxes(state_rglru[:, j], 0, 1), nch, axis=1)
            (ys,) = _rg_core(cfg, proj, NP, DB * nch, DT // nch, nch, h0s, rgw, j, False)
            mix = jnp.concatenate([yp, ys], axis=0)
            new_states.append(jnp.swapaxes(st, 0, 1))
            w_out, wl = w_rg_out, j
        else:
            qkv = _qkv(cfg, h, w_qkv, g_q, g_k, rope, j)
            kcols = slice(cfg.n_heads * hd, (cfg.n_heads + KV) * hd)
            vcols = slice((cfg.n_heads + KV) * hd, (cfg.n_heads + 2 * KV) * hd)
            new_ks.append(qkv[:NP, kcols].reshape(B, T, KV, hd))
            new_vs.append(qkv[:NP, vcols].reshape(B, T, KV, hd))
            op = _attention(cfg, qkv, 0, B, T, T, None, j, "attn_ctx")
            os_ = _attention(cfg, qkv, NP, DB, DT, cfg.tq, (ck, cv), j, "attn_lat")
            mix = jnp.concatenate([op, os_], axis=0)
            w_out, wl = w_o, j
        x, f = _epi(cfg, mix, w_out, wl, x, mod, l, 2, g_post_mix,
                    (g_pre_ffn, l, l, 3, 4), "mix_out_proj")
        hdn = _wres_call(_ffn1_kernel, cfg, f, w_ff1, l, cfg.tn_ff, _BF, "ffn_up")
        nxt = (g_pre_mix, l + 1, l + 1, 0, 1) if l + 1 < L else None
        x, h = _epi(cfg, hdn, w_ff2, l, x, mod, l, 5, g_post_ffn, nxt, "ffn_down")

    return (x[:NP].reshape(B, T, D), x[NP:].reshape(DB, DT, D),
            jnp.stack(new_states, axis=1), jnp.stack(new_ks, axis=1), jnp.stack(new_vs, axis=1))


def kernel(x_prompt, x_sample, state_rglru, cache_k, cache_v, c, c_ctx, w_mod, b_mod, g_pre_mix, g_post_mix, g_pre_ffn, g_post_ffn, w_qkv, g_q, g_k, w_o, w_rg_in, rg_conv_w, rg_conv_b, w_rg_a, b_rg_a, w_rg_x, b_rg_x, rg_lambda, w_rg_out, w_ff1, w_ff2):
    return _forward(_CFG, x_prompt, x_sample, state_rglru, cache_k, cache_v, c, c_ctx,
                    w_mod, b_mod, g_pre_mix, g_post_mix, g_pre_ffn, g_post_ffn,
                    w_qkv, g_q, g_k, w_o,
                    w_rg_in, rg_conv_w, rg_conv_b, w_rg_a, b_rg_a, w_rg_x, b_rg_x, rg_lambda, w_rg_out,
                    w_ff1, w_ff2)
```

```python
import functools
import math
from typing import NamedTuple

import jax
import jax.numpy as jnp
from jax import lax
from jax.experimental import pallas as pl
from jax.experimental.pallas import tpu as pltpu

_BF = jnp.bfloat16
_F32 = jnp.float32

_EPS = 1e-6
_RG_C = 8.0
_ROPE_THETA = 10000.0
_LOG2E = 1.4426950408889634
_SCAN_STEPS = 4
_CONV_W = 4
_LANES = 128
_SUBLANES = 8
_MIB = 1024 * 1024


class _Cfg(NamedTuple):
    d_model: int
    batch: int
    seq: int
    depth: int
    dec_batch: int
    dec_seq: int
    past_len: int
    grid_w: int
    n_heads: int
    n_kv_heads: int
    rg_blocks: int
    d_ff: int
    dec_chunks: int
    tm: int
    tn_in: int
    tn_qkv: int
    tn_ff: int
    tm_epi: int
    tk: int
    n_epi: int
    tn_mod: int
    tq: int
    rg_rows: int
    rg_lanes: int
    epi_rows: int


_CFG = _Cfg(d_model=2048, batch=16, seq=256, depth=4, dec_batch=2, dec_seq=2048, past_len=512,
            grid_w=64, n_heads=16, n_kv_heads=4, rg_blocks=16, d_ff=8192, dec_chunks=4,
            tm=512, tn_in=1024, tn_qkv=512, tn_ff=1024, tm_epi=1024, tk=1024, n_epi=4, tn_mod=1024, tq=256,
            rg_rows=256, rg_lanes=256, epi_rows=64)


def _params(sem, vmem_mib):
    return pltpu.CompilerParams(dimension_semantics=sem, vmem_limit_bytes=vmem_mib * _MIB)


def _rms(x, g):
    return x * lax.rsqrt(jnp.mean(x * x, axis=-1, keepdims=True) + _EPS) * g


def _sigmoid(z):
    return 0.5 * jnp.tanh(0.5 * z) + 0.5


def _mod_row(cfg, tm):
    npb = cfg.batch * cfg.seq // tm
    bps = cfg.dec_seq // tm
    return lambda i: jnp.where(i < npb, 0, 1 + (i - npb) // bps)


def _mod_kernel(c_ref, w_ref, b_ref, o_ref):
    s = jax.nn.silu(c_ref[...]).astype(_BF)
    o_ref[...] = jnp.dot(s, w_ref[...].astype(_BF), preferred_element_type=_F32) + b_ref[...]


def _modulation(cfg, cvec, w_mod, b_mod):
    L, D, N = w_mod.shape
    tn = cfg.tn_mod
    return pl.pallas_call(
        _mod_kernel,
        grid=(L, N // tn),
        in_specs=[pl.BlockSpec((_SUBLANES, D), lambda l, j: (0, 0)),
                  pl.BlockSpec((None, D, tn), lambda l, j: (l, 0, j)),
                  pl.BlockSpec((None, 1, tn), lambda l, j: (l, 0, j))],
        out_specs=pl.BlockSpec((None, _SUBLANES, tn), lambda l, j: (l, 0, j)),
        out_shape=jax.ShapeDtypeStruct((L, _SUBLANES, N), _F32),
        compiler_params=_params(("arbitrary", "arbitrary"), 40),
        name="modulation",
    )(cvec, w_mod, b_mod.reshape(L, 1, N))


def _prenorm_kernel(x_ref, g_ref, sh_ref, sc_ref, o_ref):
    o_ref[...] = (_rms(x_ref[...], g_ref[...]) * (1.0 + sc_ref[...]) + sh_ref[...]).astype(_BF)


def _prenorm(cfg, x, g, mod, l):
    M, D = x.shape
    tm = cfg.tm
    row = _mod_row(cfg, tm)
    mspec = lambda idx: pl.BlockSpec((None, None, None, 1, D), lambda i: (l, row(i), idx, 0, 0))
    return pl.pallas_call(
        _prenorm_kernel,
        grid=(M // tm,),
        in_specs=[pl.BlockSpec((tm, D), lambda i: (i, 0)),
                  pl.BlockSpec((None, 1, D), lambda i: (l, 0, 0)),
                  mspec(0), mspec(1)],
        out_specs=pl.BlockSpec((tm, D), lambda i: (i, 0)),
        out_shape=jax.ShapeDtypeStruct((M, D), _BF),
        compiler_params=_params(("arbitrary",), 32),
        name="prenorm",
    )(x, g, mod, mod)


def _wres_dot(a_ref, w_ref, wbf_ref):
    @pl.when(pl.program_id(1) == 0)
    def _cast():
        wbf_ref[...] = w_ref[...].astype(_BF)
    return jnp.dot(a_ref[...], wbf_ref[...], preferred_element_type=_F32)


def _inproj_kernel(a_ref, w_ref, o_ref, wbf_ref):
    o_ref[...] = _wres_dot(a_ref, w_ref, wbf_ref)


def _ffn1_kernel(a_ref, w_ref, o_ref, wbf_ref):
    z = jnp.maximum(_wres_dot(a_ref, w_ref, wbf_ref), 0.0)
    o_ref[...] = (z * z).astype(_BF)


def _wres_call(kernel, cfg, a, w, l, tn, out_dtype, name, extra_in=(), extra_specs=()):
    M, K = a.shape
    N = w.shape[-1]
    tm = cfg.tm
    return pl.pallas_call(
        kernel,
        grid=(N // tn, M // tm),
        in_specs=[pl.BlockSpec((tm, K), lambda j, i: (i, 0)),
                  pl.BlockSpec((None, K, tn), lambda j, i: (l, 0, j)),
                  *extra_specs],
        out_specs=pl.BlockSpec((tm, tn), lambda j, i: (i, j)),
        out_shape=jax.ShapeDtypeStruct((M, N), out_dtype),
        scratch_shapes=[pltpu.VMEM((K, tn), _BF)],
        compiler_params=_params(("arbitrary", "arbitrary"), 48),
        name=name,
    )(a, w, *extra_in)


def _rope(z, cos, sin, first_half):
    hd = z.shape[-1]
    quarter = hd // 4
    partner = jnp.where(first_half, pltpu.roll(z, hd - quarter, 1), pltpu.roll(z, quarter, 1))
    return z * cos + partner * sin


def _qkv_kernel(a_ref, w_ref, gq_ref, gk_ref, cos_ref, sin_ref, o_ref, wbf_ref, *,
                n_q_tiles, n_prompt_blocks, hd):
    j = pl.program_id(0)
    i = pl.program_id(1)
    acc = _wres_dot(a_ref, w_ref, wbf_ref)
    heads = acc.shape[-1] // hd
    normed = j <= n_q_tiles
    g = jnp.where(j < n_q_tiles, gq_ref[...], gk_ref[...])

    @pl.when(jnp.logical_not(normed))
    def _v():
        o_ref[...] = acc

    @pl.when(jnp.logical_and(normed, i < n_prompt_blocks))
    def _context():
        for h in range(heads):
            o_ref[:, h * hd:(h + 1) * hd] = _rms(acc[:, h * hd:(h + 1) * hd], g)

    @pl.when(jnp.logical_and(normed, i >= n_prompt_blocks))
    def _latent():
        cos = cos_ref[...]
        sin = sin_ref[...]
        lane = lax.broadcasted_iota(jnp.int32, cos.shape, 1)
        first_half = (lane % (hd // 2)) < (hd // 4)
        for h in range(heads):
            z = _rms(acc[:, h * hd:(h + 1) * hd], g)
            o_ref[:, h * hd:(h + 1) * hd] = _rope(z, cos, sin, first_half)


def _rope_tables(cfg):
    hd = cfg.d_model // cfg.n_heads
    pairs = hd // 4
    n = cfg.dec_seq
    rows = n // cfg.grid_w
    row = jnp.broadcast_to(jnp.arange(rows, dtype=_F32)[:, None], (rows, cfg.grid_w)).reshape(-1)
    col = jnp.broadcast_to(jnp.arange(cfg.grid_w, dtype=_F32)[None, :], (rows, cfg.grid_w)).reshape(-1)
    inv_freq = _ROPE_THETA ** (-jnp.arange(pairs, dtype=_F32) / pairs)
    ang_r = row[:, None] * inv_freq[None, :]
    ang_c = col[:, None] * inv_freq[None, :]
    cos = jnp.concatenate([jnp.cos(ang_r), jnp.cos(ang_r), jnp.cos(ang_c), jnp.cos(ang_c)], axis=-1)
    sin = jnp.concatenate([-jnp.sin(ang_r), jnp.sin(ang_r), -jnp.sin(ang_c), jnp.sin(ang_c)], axis=-1)
    return cos, sin


def _qkv(cfg, h, w_qkv, g_q, g_k, rope, j):
    hd = cfg.d_model // cfg.n_heads
    tm, tn = cfg.tm, cfg.tn_qkv
    npb = cfg.batch * cfg.seq // tm
    bps = cfg.dec_seq // tm
    assert (cfg.n_kv_heads * hd) == tn, "one column tile must hold exactly the k heads"
    tab = pl.BlockSpec((tm, hd), lambda jj, i: (jnp.maximum(i - npb, 0) % bps, 0))
    gain = pl.BlockSpec((None, 1, hd), lambda jj, i: (j, 0, 0))
    kern = functools.partial(_qkv_kernel, n_q_tiles=cfg.n_heads * hd // tn, n_prompt_blocks=npb, hd=hd)
    return _wres_call(kern, cfg, h, w_qkv, j, tn, _F32, "qkv_proj",
                      extra_in=(g_q, g_k, *rope), extra_specs=(gain, gain, tab, tab))


def _epi_kernel(*refs, nk, te, with_next, rows):
    if with_next:
        a_ref, w_ref, x_ref, gate_ref, gpost_ref, gnext_ref, sh_ref, sc_ref, xo_ref, ho_ref, acc_ref = refs
    else:
        a_ref, w_ref, x_ref, gate_ref, gpost_ref, xo_ref, acc_ref = refs
    k = pl.program_id(1)

    @pl.when(k == 0)
    def _zero():
        acc_ref[...] = jnp.zeros(acc_ref.shape, _F32)

    @pl.when(k < nk)
    def _accumulate():
        acc_ref[...] += jnp.dot(a_ref[...], w_ref[...].astype(_BF), preferred_element_type=_F32)

    @pl.when(k >= nk)
    def _epilogue():
        base = (k - nk) * te
        gate = gate_ref[...]
        gpost = gpost_ref[...]

        def chunk(c, carry):
            r = pl.ds(pl.multiple_of(c * rows, rows), rows)
            ra = pl.ds(pl.multiple_of(base + c * rows, rows), rows)
            xn = x_ref[r, :] + gate * _rms(acc_ref[ra, :], gpost)
            xo_ref[r, :] = xn
            if with_next:
                ho_ref[r, :] = (_rms(xn, gnext_ref[...]) * (1.0 + sc_ref[...]) + sh_ref[...]).astype(_BF)
            return carry

        lax.fori_loop(0, te // rows, chunk, 0)


def _epi(cfg, a, w, wl, x, mod, l, gate_idx, gpost, nxt, name):
    M, K = a.shape
    D = w.shape[-1]
    tm, tk, ne = cfg.tm_epi, cfg.tk, cfg.n_epi
    te = tm // ne
    nk = K // tk
    ni = M // tm
    row = _mod_row(cfg, tm)
    ai = lambda i, k: jnp.where(k < nk, i, jnp.minimum(i + 1, ni - 1))
    ak = lambda i, k: jnp.where(k < nk, k, jnp.where(i + 1 < ni, 0, nk - 1))
    mspec = lambda ml, idx: pl.BlockSpec((None, None, None, 1, D), lambda i, k: (ml, row(i), idx, 0, 0))
    gspec = lambda gl: pl.BlockSpec((None, 1, D), lambda i, k: (gl, 0, 0))
    xspec = pl.BlockSpec((te, D), lambda i, k: (i * ne + jnp.clip(k - nk, 0, ne - 1), 0))
    in_specs = [pl.BlockSpec((tm, tk), lambda i, k: (ai(i, k), ak(i, k))),
                pl.BlockSpec((None, tk, D), lambda i, k: (wl, ak(i, k), 0)),
                xspec, mspec(l, gate_idx), gspec(l)]
    args = [a, w, x, mod, gpost]
    out_specs = [xspec]
    out_shape = [jax.ShapeDtypeStruct((M, D), _F32)]
    if nxt is not None:
        gnext, gl, ml, sh_idx, sc_idx = nxt
        in_specs += [gspec(gl), mspec(ml, sh_idx), mspec(ml, sc_idx)]
        args += [gnext, mod, mod]
        out_specs.append(xspec)
        out_shape.append(jax.ShapeDtypeStruct((M, D), _BF))
    out = pl.pallas_call(
        functools.partial(_epi_kernel, nk=nk, te=te, with_next=nxt is not None, rows=cfg.epi_rows),
        grid=(ni, nk + ne),
        in_specs=in_specs,
        out_specs=out_specs,
        out_shape=out_shape,
        scratch_shapes=[pltpu.VMEM((tm, D), _F32)],
        compiler_params=_params(("arbitrary", "arbitrary"), 56),
        name=name,
    )(*args)
    return out if nxt is not None else (out[0], None)


def _rg_kernel(*refs, S, Tc, nchunk, RB, want_final):
    (xb_ref, gt_ref, cw_ref, cb_ref, wa_ref, ba_ref, wx_ref, bx_ref, lam_ref, h0_ref), rest = refs[:10], refs[10:]
    if want_final:
        yg_ref, hfin_ref, xp_s, a_s, u_s, cin_s = rest
    else:
        yg_ref, xp_s, a_s, u_s, cin_s = rest
    R = S * Tc
    T = Tc * nchunk
    C = xb_ref.shape[-1]
    nlt = C // _LANES
    pad = _SUBLANES
    lanes = [slice(n * _LANES, (n + 1) * _LANES) for n in range(nlt)]

    nseq = R // T
    span = T + 2 * pad

    def fill(q, carry):
        base = pl.multiple_of(q * span, pad)
        xp_s[pl.ds(base, pad), :] = jnp.zeros((pad, C), _F32)
        xp_s[pl.ds(base + pad, T), :] = xb_ref[pl.ds(pl.multiple_of(q * T, T), T), :]
        xp_s[pl.ds(base + pad + T, pad), :] = jnp.zeros((pad, C), _F32)
        return carry

    lax.fori_loop(0, nseq, fill, 0)

    wa = (0.5 * wa_ref[...]).astype(_BF)
    wx = (0.5 * wx_ref[...]).astype(_BF)
    ba = 0.5 * ba_ref[...]
    bx = 0.5 * bx_ref[...]
    nceh = (0.5 * _RG_C) * jax.nn.softplus(-lam_ref[...])
    cw = cw_ref[...]
    cb = cb_ref[...]

    def gates(c, carry):
        r0 = pl.multiple_of(c * RB, RB)
        nwin = RB + 2 * pad
        s = r0 // Tc
        t0 = r0 % Tc
        dst = pl.ds(t0 * S + s, RB, stride=S)
        w0 = pl.multiple_of(r0 + (r0 // T) * 2 * pad, pad)
        for n, ln in enumerate(lanes):
            win = xp_s[pl.ds(w0, nwin), ln]
            x0 = win[pad:pad + RB]
            xm1 = pltpu.roll(win, 1, 0)[pad:pad + RB]
            xp1 = pltpu.roll(win, nwin - 1, 0)[pad:pad + RB]
            xp2 = pltpu.roll(win, nwin - 2, 0)[pad:pad + RB]
            xc = cb[:, ln] + xm1 * cw[0:1, ln] + x0 * cw[1:2, ln] + xp1 * cw[2:3, ln] + xp2 * cw[3:4, ln]
            xh = 0.5 * xc
            x16 = xc.astype(_BF)
            for e in range(2):
                tr = jnp.tanh(jnp.dot(x16, wa[e, n], preferred_element_type=_F32) + ba[e][:, ln])
                ti = jnp.tanh(jnp.dot(x16, wx[e, n], preferred_element_type=_F32) + bx[e][:, ln])
                nla = nceh[e][:, ln] * tr + nceh[e][:, ln]
                a = jnp.exp2(nla * (-_LOG2E))
                a_s[e, n, dst, :] = a
                w = jnp.tanh(nla) * (a * a + 1.0)
                sqrt_w = jnp.where(w > 0.0, w * lax.rsqrt(w), 0.0)
                u_s[e, n, dst, :] = sqrt_w * (ti * xh + xh)
        return carry

    lax.fori_loop(0, R // RB, gates, 0)

    chunked = nchunk > 1

    chains = [(e, n) for e in range(2) for n in range(nlt)]
    steps = _SCAN_STEPS

    def scan_block(tb, carry):
        h, p = list(carry[0]), list(carry[1])
        rows = [(pl.ds(pl.multiple_of((tb * steps + k) * S, S), S),
                 pl.ds(pl.multiple_of((Tc - 1 - tb * steps - k) * S, S), S)) for k in range(steps)]
        av = [[a_s[e, n, rows[k][e], :] for k in range(steps)] for e, n in chains]
        uv = [[u_s[e, n, rows[k][e], :] for k in range(steps)] for e, n in chains]
        hs, ps = [], []
        for k in range(steps):
            for i in range(len(chains)):
                h[i] = av[i][k] * h[i] + uv[i][k]
                hs.append(h[i])
                if chunked:
                    p[i] = av[i][k] * p[i]
                    ps.append(p[i])
        for k in range(steps):
            for i, (e, n) in enumerate(chains):
                u_s[e, n, rows[k][e], :] = hs[k * len(chains) + i]
                if chunked:
                    a_s[e, n, rows[k][e], :] = ps[k * len(chains) + i]
        return tuple(h), tuple(p)

    zeros = jnp.zeros((S, _LANES), _F32)
    ones = jnp.ones((S, _LANES), _F32)
    if chunked:
        init = ((zeros,) * (2 * nlt), (ones,) * (2 * nlt))
    else:
        init = (tuple(h0_ref[e][:, ln] for e in range(2) for ln in lanes), ())
    h, p = lax.fori_loop(0, Tc // steps, scan_block, init)

    if want_final:
        for e in range(2):
            for n, ln in enumerate(lanes):
                hfin_ref[e, :, ln] = h[e * nlt + n]

    if chunked:
        chunk_id = lax.broadcasted_iota(jnp.int32, (S, _LANES), 0) % nchunk
        edge = (chunk_id == 0, chunk_id == nchunk - 1)
        shift = (1, S - 1)
        for e in range(2):
            for n, ln in enumerate(lanes):
                h0 = h0_ref[e][:, ln]
                cin = jnp.where(edge[e], h0, 0.0)
                for _ in range(nchunk - 1):
                    cin = jnp.where(edge[e], h0, pltpu.roll(h[e * nlt + n] + p[e * nlt + n] * cin, shift[e], 0))
                cin_s[e, n] = cin

    def emit(c, carry):
        r0 = pl.multiple_of(c * RB, RB)
        s = r0 // Tc
        t0 = r0 % Tc
        src = pl.ds(t0 * S + s, RB, stride=S)
        r = pl.ds(r0, RB)
        for n, ln in enumerate(lanes):
            y = []
            for e in range(2):
                ye = u_s[e, n, src, :]
                if chunked:
                    ye = ye + a_s[e, n, src, :] * cin_s[e, n, pl.ds(s, 1), :]
                y.append(ye)
            yg_ref[r, ln] = ((y[0] + y[1]) * jax.nn.gelu(gt_ref[r, ln])).astype(_BF)
        return carry

    lax.fori_loop(0, R // RB, emit, 0)


def _rg_core(cfg, proj, row_off, S, Tc, nchunk, h0, rgw, j, want_final):
    conv_w, conv_b, w_a, b_a, w_x, b_x, lam = rgw
    rnn = proj.shape[1] // 2
    C = cfg.rg_lanes
    blk = rnn // cfg.rg_blocks
    assert blk == _LANES, "one gate block per lane tile"
    nct = rnn // C
    R = S * Tc
    RB = min(cfg.rg_rows, Tc)
    rb0 = row_off // R
    nrg = conv_w.shape[0]
    vec = lambda arr: arr.reshape(nrg, 2, 1, rnn)
    vspec = pl.BlockSpec((None, 2, 1, C), lambda ct: (j, 0, 0, ct))
    wspec = pl.BlockSpec((None, 2, C // blk, blk, blk), lambda ct: (j, 0, ct, 0, 0))
    out_specs = [pl.BlockSpec((R, C), lambda ct: (0, ct))]
    out_shape = [jax.ShapeDtypeStruct((R, rnn), _BF)]
    if want_final:
        out_specs.append(pl.BlockSpec((2, S, C), lambda ct: (0, 0, ct)))
        out_shape.append(jax.ShapeDtypeStruct((2, S, rnn), _F32))
    return pl.pallas_call(
        functools.partial(_rg_kernel, S=S, Tc=Tc, nchunk=nchunk, RB=RB, want_final=want_final),
        grid=(nct,),
        in_specs=[pl.BlockSpec((R, C), lambda ct: (rb0, nct + ct)),
                  pl.BlockSpec((R, C), lambda ct: (rb0, ct)),
                  pl.BlockSpec((None, _CONV_W, C), lambda ct: (j, 0, ct)),
                  pl.BlockSpec((None, 1, C), lambda ct: (j, 0, ct)),
                  wspec, vspec, wspec, vspec, vspec,
                  pl.BlockSpec((2, S, C), lambda ct: (0, 0, ct))],
        out_specs=out_specs,
        out_shape=out_shape,
        scratch_shapes=[pltpu.VMEM((R + (R // (Tc * nchunk)) * 2 * _SUBLANES, C), _F32),
                        pltpu.VMEM((2, C // _LANES, R, _LANES), _F32),
                        pltpu.VMEM((2, C // _LANES, R, _LANES), _F32),
                        pltpu.VMEM((2, C // _LANES, S, _LANES), _F32)],
        compiler_params=_params(("arbitrary",), 56),
        name="rg_core_ctx" if want_final else "rg_core_lat",
    )(proj, proj, conv_w, conv_b.reshape(nrg, 1, rnn), w_a, vec(b_a), w_x, vec(b_x), vec(lam), h0)


def _attn_kernel(*refs, G, hd, scale, has_cache):
    if has_cache:
        q_ref, k_ref, v_ref, ck_ref, cv_ref, o_ref = refs
        kc = ck_ref[...].astype(_BF)
        vc = cv_ref[...].astype(_BF)
    else:
        q_ref, k_ref, v_ref, o_ref = refs
    k = k_ref[...].astype(_BF)
    v = v_ref[...].astype(_BF)
    nt = (((1,), (1,)), ((), ()))
    for g in range(G):
        q = q_ref[:, g * hd:(g + 1) * hd].astype(_BF)
        s1 = lax.dot_general(q, k, nt, preferred_element_type=_F32) * scale
        m = jnp.max(s1, axis=-1, keepdims=True)
        if has_cache:
            s2 = lax.dot_general(q, kc, nt, preferred_element_type=_F32) * scale
            m = jnp.maximum(m, jnp.max(s2, axis=-1, keepdims=True))
        p1 = jnp.exp(s1 - m)
        l = jnp.sum(p1, axis=-1, keepdims=True)
        o = jnp.dot(p1.astype(_BF), v, preferred_element_type=_F32)
        if has_cache:
            p2 = jnp.exp(s2 - m)
            l = l + jnp.sum(p2, axis=-1, keepdims=True)
            o = o + jnp.dot(p2.astype(_BF), vc, preferred_element_type=_F32)
        o_ref[:, g * hd:(g + 1) * hd] = (o / l).astype(_BF)


def _attention(cfg, qkv, row_off, B, T, tq, cache, j, name):
    hd = cfg.d_model // cfg.n_heads
    KV = cfg.n_kv_heads
    G = cfg.n_heads // KV
    nq = T // tq
    qb0 = row_off // tq
    kb0 = row_off // T
    kcol = cfg.n_heads
    vcol = cfg.n_heads + KV
    in_specs = [pl.BlockSpec((tq, G * hd), lambda b, h, qi: (qb0 + b * nq + qi, h)),
                pl.BlockSpec((T, hd), lambda b, h, qi: (kb0 + b, kcol + h)),
                pl.BlockSpec((T, hd), lambda b, h, qi: (kb0 + b, vcol + h))]
    args = [qkv, qkv, qkv]
    if cache is not None:
        ck, cv = cache
        P = ck.shape[2]
        cspec = pl.BlockSpec((None, None, P, hd), lambda b, h, qi: (b, j, 0, h))
        in_specs += [cspec, cspec]
        args += [ck, cv]
    return pl.pallas_call(
        functools.partial(_attn_kernel, G=G, hd=hd, scale=1.0 / math.sqrt(hd), has_cache=cache is not None),
        grid=(B, KV, nq),
        in_specs=in_specs,
        out_specs=pl.BlockSpec((tq, G * hd), lambda b, h, qi: (b * nq + qi, h)),
        out_shape=jax.ShapeDtypeStruct((B * T, cfg.n_heads * hd), _BF),
        compiler_params=_params(("arbitrary", "arbitrary", "arbitrary"), 48),
        name=name,
    )(*args)


def _forward(cfg, x_prompt, x_sample, state_rglru, cache_k, cache_v, c, c_ctx,
             w_mod, b_mod, g_pre_mix, g_post_mix, g_pre_ffn, g_post_ffn,
             w_qkv, g_q, g_k, w_o,
             w_rg_in, rg_conv_w, rg_conv_b, w_rg_a, b_rg_a, w_rg_x, b_rg_x, rg_lambda, w_rg_out,
             w_ff1, w_ff2):
    D = cfg.d_model
    L = cfg.depth
    B, T = cfg.batch, cfg.seq
    DB, DT = cfg.dec_batch, cfg.dec_seq
    NP = B * T
    hd = D // cfg.n_heads
    KV = cfg.n_kv_heads
    nch = cfg.dec_chunks

    x = jnp.concatenate([x_prompt.reshape(NP, D), x_sample.reshape(DB * DT, D)], axis=0)

    cvec = jnp.zeros((_SUBLANES, D), _F32).at[0].set(c_ctx).at[1:1 + DB].set(c)
    mod = _modulation(cfg, cvec, w_mod, b_mod)[:, :1 + DB].reshape(L, 1 + DB, 6, 1, D)

    gains = lambda g: g.reshape(g.shape[0], 1, g.shape[1])
    g_pre_mix, g_post_mix, g_pre_ffn, g_post_ffn = map(gains, (g_pre_mix, g_post_mix, g_pre_ffn, g_post_ffn))
    g_q, g_k = gains(g_q), gains(g_k)
    rope = _rope_tables(cfg)
    ck = cache_k.reshape(DB, -1, cfg.past_len, KV * hd)
    cv = cache_v.reshape(DB, -1, cfg.past_len, KV * hd)

    h = _prenorm(cfg, x, g_pre_mix, mod, 0)
    new_states, new_ks, new_vs = [], [], []
    for l in range(L):
        j = l // 2
        if l % 2 == 0:
            proj = _wres_call(_inproj_kernel, cfg, h, w_rg_in, j, cfg.tn_in, _F32, "rg_in_proj")
            rgw = (rg_conv_w, rg_conv_b, w_rg_a, b_rg_a, w_rg_x, b_rg_x, rg_lambda)
            h0p = jnp.zeros((2, B, D), _F32)
            yp, st = _rg_core(cfg, proj, 0, B, T, 1, h0p, rgw, j, True)
            h0s = jnp.repeat(jnp.swapaxes(state_rglru[:, j], 0, 1), nch, axis=1)
            (ys,) = _rg_core(cfg, proj, NP, DB * nch, DT // nch, nch, h0s, rgw, j, False)
            mix = jnp.concatenate([yp, ys], axis=0)
            new_states.append(jnp.swapaxes(st, 0, 1))
            w_out, wl = w_rg_out, j
        else:
            qkv = _qkv(cfg, h, w_qkv, g_q, g_k, rope, j)
            kcols = slice(cfg.n_heads * hd, (cfg.n_heads + KV) * hd)
            vcols = slice((cfg.n_heads + KV) * hd, (cfg.n_heads + 2 * KV) * hd)
            new_ks.append(qkv[:NP, kcols].reshape(B, T, KV, hd))
            new_vs.append(qkv[:NP, vcols].reshape(B, T, KV, hd))
            op = _attention(cfg, qkv, 0, B, T, T, None, j, "attn_ctx")
            os_ = _attention(cfg, qkv, NP, DB, DT, cfg.tq, (ck, cv), j, "attn_lat")
            mix = jnp.concatenate([op, os_], axis=0)
            w_out, wl = w_o, j
        x, f = _epi(cfg, mix, w_out, wl, x, mod, l, 2, g_post_mix,
                    (g_pre_ffn, l, l, 3, 4), "mix_out_proj")
        hdn = _wres_call(_ffn1_kernel, cfg, f, w_ff1, l, cfg.tn_ff, _BF, "ffn_up")
        nxt = (g_pre_mix, l + 1, l + 1, 0, 1) if l + 1 < L else None
        x, h = _epi(cfg, hdn, w_ff2, l, x, mod, l, 5, g_post_ffn, nxt, "ffn_down")

    return (x[:NP].reshape(B, T, D), x[NP:].reshape(DB, DT, D),
            jnp.stack(new_states, axis=1), jnp.stack(new_ks, axis=1), jnp.stack(new_vs, axis=1))


def kernel(x_prompt, x_sample, state_rglru, cache_k, cache_v, c, c_ctx, w_mod, b_mod, g_pre_mix, g_post_mix, g_pre_ffn, g_post_ffn, w_qkv, g_q, g_k, w_o, w_rg_in, rg_conv_w, rg_conv_b, w_rg_a, b_rg_a, w_rg_x, b_rg_x, rg_lambda, w_rg_out, w_ff1, w_ff2):
    return _forward(_CFG, x_prompt, x_sample, state_rglru, cache_k, cache_v, c, c_ctx,
                    w_mod, b_mod, g_pre_mix, g_post_mix, g_pre_ffn, g_post_ffn,
                    w_qkv, g_q, g_k, w_o,
                    w_rg_in, rg_conv_w, rg_conv_b, w_rg_a, b_rg_a, w_rg_x, b_rg_x, rg_lambda, w_rg_out,
                    w_ff1, w_ff2)
```

```python
import functools
import math
from typing import NamedTuple

import jax
import jax.numpy as jnp
from jax import lax
from jax.experimental import pallas as pl
from jax.experimental.pallas import tpu as pltpu

_BF = jnp.bfloat16
_F32 = jnp.float32

_EPS = 1e-6
_RG_C = 8.0
_ROPE_THETA = 10000.0
_LOG2E = 1.4426950408889634
_SCAN_STEPS = 4
_CONV_W = 4
_LANES = 128
_SUBLANES = 8
_MIB = 1024 * 1024


class _Cfg(NamedTuple):
    d_model: int
    batch: int
    seq: int
    depth: int
    dec_batch: int
    dec_seq: int
    past_len: int
    grid_w: int
    n_heads: int
    n_kv_heads: int
    rg_blocks: int
    d_ff: int
    dec_chunks: int
    tm: int
    tm_mm: int
    tn_in: int
    tn_qkv: int
    tn_ff: int
    tm_epi: int
    tk: int
    n_epi: int
    tn_mod: int
    tq: int
    rg_rows: int
    rg_lanes: int
    epi_rows: int


_CFG = _Cfg(d_model=2048, batch=16, seq=256, depth=4, dec_batch=2, dec_seq=2048, past_len=512,
            grid_w=64, n_heads=16, n_kv_heads=4, rg_blocks=16, d_ff=8192, dec_chunks=4,
            tm=512, tm_mm=1024, tn_in=1024, tn_qkv=512, tn_ff=1024, tm_epi=1024, tk=1024, n_epi=4, tn_mod=1024, tq=256,
            rg_rows=256, rg_lanes=256, epi_rows=64)


def _params(sem, vmem_mib):
    return pltpu.CompilerParams(dimension_semantics=sem, vmem_limit_bytes=vmem_mib * _MIB)


def _rms(x, g):
    return x * lax.rsqrt(jnp.mean(x * x, axis=-1, keepdims=True) + _EPS) * g


def _sigmoid(z):
    return 0.5 * jnp.tanh(0.5 * z) + 0.5


def _mod_row(cfg, tm):
    npb = cfg.batch * cfg.seq // tm
    bps = cfg.dec_seq // tm
    return lambda i: jnp.where(i < npb, 0, 1 + (i - npb) // bps)


def _mod_kernel(c_ref, w_ref, b_ref, o_ref):
    s = jax.nn.silu(c_ref[...]).astype(_BF)
    o_ref[...] = jnp.dot(s, w_ref[...].astype(_BF), preferred_element_type=_F32) + b_ref[...]


def _modulation(cfg, cvec, w_mod, b_mod):
    L, D, N = w_mod.shape
    tn = cfg.tn_mod
    return pl.pallas_call(
        _mod_kernel,
        grid=(L, N // tn),
        in_specs=[pl.BlockSpec((_SUBLANES, D), lambda l, j: (0, 0)),
                  pl.BlockSpec((None, D, tn), lambda l, j: (l, 0, j)),
                  pl.BlockSpec((None, 1, tn), lambda l, j: (l, 0, j))],
        out_specs=pl.BlockSpec((None, _SUBLANES, tn), lambda l, j: (l, 0, j)),
        out_shape=jax.ShapeDtypeStruct((L, _SUBLANES, N), _F32),
        compiler_params=_params(("arbitrary", "arbitrary"), 40),
        name="modulation",
    )(cvec, w_mod, b_mod.reshape(L, 1, N))


def _prenorm_kernel(xp_ref, xs_ref, g_ref, sh_ref, sc_ref, x_ref, h_ref, *, n_prompt_blocks):
    def emit(src_ref):
        x = src_ref[...]
        x_ref[...] = x
        h_ref[...] = (_rms(x, g_ref[...]) * (1.0 + sc_ref[...]) + sh_ref[...]).astype(_BF)

    i = pl.program_id(0)
    pl.when(i < n_prompt_blocks)(lambda: emit(xp_ref))
    pl.when(i >= n_prompt_blocks)(lambda: emit(xs_ref))


def _prenorm(cfg, xp, xs, g, mod, l):
    D = xp.shape[1]
    M = xp.shape[0] + xs.shape[0]
    tm = cfg.tm
    npb = xp.shape[0] // tm
    row = _mod_row(cfg, tm)
    mspec = lambda idx: pl.BlockSpec((None, None, None, 1, D), lambda i: (l, row(i), idx, 0, 0))
    out = pl.BlockSpec((tm, D), lambda i: (i, 0))
    return pl.pallas_call(
        functools.partial(_prenorm_kernel, n_prompt_blocks=npb),
        grid=(M // tm,),
        in_specs=[pl.BlockSpec((tm, D), lambda i: (jnp.minimum(i, npb - 1), 0)),
                  pl.BlockSpec((tm, D), lambda i: (jnp.maximum(i - npb, 0), 0)),
                  pl.BlockSpec((None, 1, D), lambda i: (l, 0, 0)),
                  mspec(0), mspec(1)],
        out_specs=[out, out],
        out_shape=[jax.ShapeDtypeStruct((M, D), _F32), jax.ShapeDtypeStruct((M, D), _BF)],
        compiler_params=_params(("arbitrary",), 32),
        name="prenorm",
    )(xp, xs, g, mod, mod)


def _wres_dot(a_ref, w_ref, wbf_ref):
    @pl.when(pl.program_id(1) == 0)
    def _cast():
        wbf_ref[...] = w_ref[...].astype(_BF)
    return jnp.dot(a_ref[...], wbf_ref[...], preferred_element_type=_F32)


def _inproj_kernel(a_ref, w_ref, o_ref, wbf_ref):
    o_ref[...] = _wres_dot(a_ref, w_ref, wbf_ref)


def _ffn1_kernel(a_ref, w_ref, o_ref, wbf_ref):
    z = jnp.maximum(_wres_dot(a_ref, w_ref, wbf_ref), 0.0)
    o_ref[...] = (z * z).astype(_BF)


def _wres_call(kernel, cfg, a, w, l, tn, out_dtype, name, extra_in=(), extra_specs=()):
    M, K = a.shape
    N = w.shape[-1]
    tm = cfg.tm_mm
    return pl.pallas_call(
        kernel,
        grid=(N // tn, M // tm),
        in_specs=[pl.BlockSpec((tm, K), lambda j, i: (i, 0)),
                  pl.BlockSpec((None, K, tn), lambda j, i: (l, 0, j)),
                  *extra_specs],
        out_specs=pl.BlockSpec((tm, tn), lambda j, i: (i, j)),
        out_shape=jax.ShapeDtypeStruct((M, N), out_dtype),
        scratch_shapes=[pltpu.VMEM((K, tn), _BF)],
        compiler_params=_params(("arbitrary", "arbitrary"), 48),
        name=name,
    )(a, w, *extra_in)


def _rope(z, cos, sin, first_half):
    hd = z.shape[-1]
    quarter = hd // 4
    partner = jnp.where(first_half, pltpu.roll(z, hd - quarter, 1), pltpu.roll(z, quarter, 1))
    return z * cos + partner * sin


def _qkv_kernel(a_ref, w_ref, gq_ref, gk_ref, cos_ref, sin_ref, o_ref, wbf_ref, *,
                n_q_tiles, n_prompt_blocks, hd):
    j = pl.program_id(0)
    i = pl.program_id(1)
    acc = _wres_dot(a_ref, w_ref, wbf_ref)
    heads = acc.shape[-1] // hd
    normed = j <= n_q_tiles
    g = jnp.where(j < n_q_tiles, gq_ref[...], gk_ref[...])

    @pl.when(jnp.logical_not(normed))
    def _v():
        o_ref[...] = acc

    @pl.when(jnp.logical_and(normed, i < n_prompt_blocks))
    def _context():
        for h in range(heads):
            o_ref[:, h * hd:(h + 1) * hd] = _rms(acc[:, h * hd:(h + 1) * hd], g)

    @pl.when(jnp.logical_and(normed, i >= n_prompt_blocks))
    def _latent():
        cos = cos_ref[...]
        sin = sin_ref[...]
        lane = lax.broadcasted_iota(jnp.int32, cos.shape, 1)
        first_half = (lane % (hd // 2)) < (hd // 4)
        for h in range(heads):
            z = _rms(acc[:, h * hd:(h + 1) * hd], g)
            o_ref[:, h * hd:(h + 1) * hd] = _rope(z, cos, sin, first_half)


def _rope_tables(cfg):
    hd = cfg.d_model // cfg.n_heads
    pairs = hd // 4
    n = cfg.dec_seq
    rows = n // cfg.grid_w
    row = jnp.broadcast_to(jnp.arange(rows, dtype=_F32)[:, None], (rows, cfg.grid_w)).reshape(-1)
    col = jnp.broadcast_to(jnp.arange(cfg.grid_w, dtype=_F32)[None, :], (rows, cfg.grid_w)).reshape(-1)
    inv_freq = _ROPE_THETA ** (-jnp.arange(pairs, dtype=_F32) / pairs)
    ang_r = row[:, None] * inv_freq[None, :]
    ang_c = col[:, None] * inv_freq[None, :]
    cos = jnp.concatenate([jnp.cos(ang_r), jnp.cos(ang_r), jnp.cos(ang_c), jnp.cos(ang_c)], axis=-1)
    sin = jnp.concatenate([-jnp.sin(ang_r), jnp.sin(ang_r), -jnp.sin(ang_c), jnp.sin(ang_c)], axis=-1)
    return cos, sin


def _qkv(cfg, h, w_qkv, g_q, g_k, rope, j):
    hd = cfg.d_model // cfg.n_heads
    tm, tn = cfg.tm_mm, cfg.tn_qkv
    npb = cfg.batch * cfg.seq // tm
    bps = cfg.dec_seq // tm
    assert (cfg.n_kv_heads * hd) == tn, "one column tile must hold exactly the k heads"
    tab = pl.BlockSpec((tm, hd), lambda jj, i: (jnp.maximum(i - npb, 0) % bps, 0))
    gain = pl.BlockSpec((None, 1, hd), lambda jj, i: (j, 0, 0))
    kern = functools.partial(_qkv_kernel, n_q_tiles=cfg.n_heads * hd // tn, n_prompt_blocks=npb, hd=hd)
    return _wres_call(kern, cfg, h, w_qkv, j, tn, _F32, "qkv_proj",
                      extra_in=(g_q, g_k, *rope), extra_specs=(gain, gain, tab, tab))


def _epi_kernel(*refs, n_a, split, nk, te, with_next, rows):
    a_refs, refs = refs[:n_a], refs[n_a:]
    if with_next:
        w_ref, x_ref, gate_ref, gpost_ref, gnext_ref, sh_ref, sc_ref, xo_ref, ho_ref, acc_ref = refs
    else:
        w_ref, x_ref, gate_ref, gpost_ref, xo_ref, acc_ref = refs
    i = pl.program_id(0)
    k = pl.program_id(1)

    @pl.when(k == 0)
    def _zero():
        acc_ref[...] = jnp.zeros(acc_ref.shape, _F32)

    def accumulate(a_ref):
        acc_ref[...] += jnp.dot(a_ref[...], w_ref[...].astype(_BF), preferred_element_type=_F32)

    if n_a == 1:
        pl.when(k < nk)(lambda: accumulate(a_refs[0]))
    else:
        pl.when(jnp.logical_and(k < nk, i < split))(lambda: accumulate(a_refs[0]))
        pl.when(jnp.logical_and(k < nk, i >= split))(lambda: accumulate(a_refs[1]))

    @pl.when(k >= nk)
    def _epilogue():
        base = (k - nk) * te
        gate = gate_ref[...]
        gpost = gpost_ref[...]

        def chunk(c, carry):
            r = pl.ds(pl.multiple_of(c * rows, rows), rows)
            ra = pl.ds(pl.multiple_of(base + c * rows, rows), rows)
            xn = x_ref[r, :] + gate * _rms(acc_ref[ra, :], gpost)
            xo_ref[r, :] = xn
            if with_next:
                ho_ref[r, :] = (_rms(xn, gnext_ref[...]) * (1.0 + sc_ref[...]) + sh_ref[...]).astype(_BF)
            return carry

        lax.fori_loop(0, te // rows, chunk, 0)


def _epi(cfg, a_parts, w, wl, x, mod, l, gate_idx, gpost, nxt, name):
    K = a_parts[0].shape[1]
    M = sum(p.shape[0] for p in a_parts)
    D = w.shape[-1]
    tm, tk, ne = cfg.tm_epi, cfg.tk, cfg.n_epi
    te = tm // ne
    nk = K // tk
    ni = M // tm
    row = _mod_row(cfg, tm)
    ai = lambda i, k: jnp.where(k < nk, i, jnp.minimum(i + 1, ni - 1))
    ak = lambda i, k: jnp.where(k < nk, k, jnp.where(i + 1 < ni, 0, nk - 1))
    mspec = lambda ml, idx: pl.BlockSpec((None, None, None, 1, D), lambda i, k: (ml, row(i), idx, 0, 0))
    gspec = lambda gl: pl.BlockSpec((None, 1, D), lambda i, k: (gl, 0, 0))
    xspec = pl.BlockSpec((te, D), lambda i, k: (i * ne + jnp.clip(k - nk, 0, ne - 1), 0))
    n1 = a_parts[0].shape[0] // tm
    if len(a_parts) == 1:
        a_specs = [pl.BlockSpec((tm, tk), lambda i, k: (ai(i, k), ak(i, k)))]
    else:
        first = lambda i, k: ai(i, k) < n1
        a_specs = [pl.BlockSpec((tm, tk), lambda i, k: (jnp.where(first(i, k), ai(i, k), n1 - 1),
                                                        jnp.where(first(i, k), ak(i, k), nk - 1))),
                   pl.BlockSpec((tm, tk), lambda i, k: (jnp.where(first(i, k), 0, ai(i, k) - n1),
                                                        jnp.where(first(i, k), 0, ak(i, k))))]
    in_specs = [*a_specs,
                pl.BlockSpec((None, tk, D), lambda i, k: (wl, ak(i, k), 0)),
                xspec, mspec(l, gate_idx), gspec(l)]
    args = [*a_parts, w, x, mod, gpost]
    out_specs = [xspec]
    out_shape = [jax.ShapeDtypeStruct((M, D), _F32)]
    if nxt is not None:
        gnext, gl, ml, sh_idx, sc_idx = nxt
        in_specs += [gspec(gl), mspec(ml, sh_idx), mspec(ml, sc_idx)]
        args += [gnext, mod, mod]
        out_specs.append(xspec)
        out_shape.append(jax.ShapeDtypeStruct((M, D), _BF))
    out = pl.pallas_call(
        functools.partial(_epi_kernel, n_a=len(a_parts), split=n1, nk=nk, te=te,
                          with_next=nxt is not None, rows=cfg.epi_rows),
        grid=(ni, nk + ne),
        in_specs=in_specs,
        out_specs=out_specs,
        out_shape=out_shape,
        scratch_shapes=[pltpu.VMEM((tm, D), _F32)],
        compiler_params=_params(("arbitrary", "arbitrary"), 56),
        name=name,
    )(*args)
    return out if nxt is not None else (out[0], None)


def _rg_kernel(*refs, S, Tc, nchunk, RB, want_final):
    (xb_ref, gt_ref, cw_ref, cb_ref, wa_ref, ba_ref, wx_ref, bx_ref, lam_ref, h0_ref), rest = refs[:10], refs[10:]
    if want_final:
        yg_ref, hfin_ref, xp_s, a_s, u_s, cin_s = rest
    else:
        yg_ref, xp_s, a_s, u_s, cin_s = rest
    R = S * Tc
    T = Tc * nchunk
    C = xb_ref.shape[-1]
    nlt = C // _LANES
    pad = _SUBLANES
    lanes = [slice(n * _LANES, (n + 1) * _LANES) for n in range(nlt)]

    nseq = R // T
    span = T + 2 * pad

    def fill(q, carry):
        base = pl.multiple_of(q * span, pad)
        xp_s[pl.ds(base, pad), :] = jnp.zeros((pad, C), _F32)
        xp_s[pl.ds(base + pad, T), :] = xb_ref[pl.ds(pl.multiple_of(q * T, T), T), :]
        xp_s[pl.ds(base + pad + T, pad), :] = jnp.zeros((pad, C), _F32)
        return carry

    lax.fori_loop(0, nseq, fill, 0)

    wa = (0.5 * wa_ref[...]).astype(_BF)
    wx = (0.5 * wx_ref[...]).astype(_BF)
    ba = 0.5 * ba_ref[...]
    bx = 0.5 * bx_ref[...]
    nceh = (0.5 * _RG_C) * jax.nn.softplus(-lam_ref[...])
    cw = cw_ref[...]
    cb = cb_ref[...]

    def gates(c, carry):
        r0 = pl.multiple_of(c * RB, RB)
        nwin = RB + 2 * pad
        s = r0 // Tc
        t0 = r0 % Tc
        dst = pl.ds(t0 * S + s, RB, stride=S)
        w0 = pl.multiple_of(r0 + (r0 // T) * 2 * pad, pad)
        for n, ln in enumerate(lanes):
            win = xp_s[pl.ds(w0, nwin), ln]
            x0 = win[pad:pad + RB]
            xm1 = pltpu.roll(win, 1, 0)[pad:pad + RB]
            xp1 = pltpu.roll(win, nwin - 1, 0)[pad:pad + RB]
            xp2 = pltpu.roll(win, nwin - 2, 0)[pad:pad + RB]
            xc = cb[:, ln] + xm1 * cw[0:1, ln] + x0 * cw[1:2, ln] + xp1 * cw[2:3, ln] + xp2 * cw[3:4, ln]
            xh = 0.5 * xc
            x16 = xc.astype(_BF)
            for e in range(2):
                tr = jnp.tanh(jnp.dot(x16, wa[e, n], preferred_element_type=_F32) + ba[e][:, ln])
                ti = jnp.tanh(jnp.dot(x16, wx[e, n], preferred_element_type=_F32) + bx[e][:, ln])
                nla = nceh[e][:, ln] * tr + nceh[e][:, ln]
                a = jnp.exp2(nla * (-_LOG2E))
                a_s[e, n, dst, :] = a
                w = jnp.tanh(nla) * (a * a + 1.0)
                sqrt_w = jnp.where(w > 0.0, w * lax.rsqrt(w), 0.0)
                u_s[e, n, dst, :] = sqrt_w * (ti * xh + xh)
        return carry

    lax.fori_loop(0, R // RB, gates, 0)

    chunked = nchunk > 1

    chains = [(e, n) for e in range(2) for n in range(nlt)]
    steps = _SCAN_STEPS

    def scan_block(tb, carry):
        h, p = list(carry[0]), list(carry[1])
        rows = [(pl.ds(pl.multiple_of((tb * steps + k) * S, S), S),
                 pl.ds(pl.multiple_of((Tc - 1 - tb * steps - k) * S, S), S)) for k in range(steps)]
        av = [[a_s[e, n, rows[k][e], :] for k in range(steps)] for e, n in chains]
        uv = [[u_s[e, n, rows[k][e], :] for k in range(steps)] for e, n in chains]
        hs, ps = [], []
        for k in range(steps):
            for i in range(len(chains)):
                h[i] = av[i][k] * h[i] + uv[i][k]
                hs.append(h[i])
                if chunked:
                    p[i] = av[i][k] * p[i]
                    ps.append(p[i])
        for k in range(steps):
            for i, (e, n) in enumerate(chains):
                u_s[e, n, rows[k][e], :] = hs[k * len(chains) + i]
                if chunked:
                    a_s[e, n, rows[k][e], :] = ps[k * len(chains) + i]
        return tuple(h), tuple(p)

    zeros = jnp.zeros((S, _LANES), _F32)
    ones = jnp.ones((S, _LANES), _F32)
    if chunked:
        init = ((zeros,) * (2 * nlt), (ones,) * (2 * nlt))
    else:
        init = (tuple(h0_ref[e][:, ln] for e in range(2) for ln in lanes), ())
    h, p = lax.fori_loop(0, Tc // steps, scan_block, init)

    if want_final:
        for e in range(2):
            for n, ln in enumerate(lanes):
                hfin_ref[e, :, ln] = h[e * nlt + n]

    if chunked:
        chunk_id = lax.broadcasted_iota(jnp.int32, (S, _LANES), 0) % nchunk
        edge = (chunk_id == 0, chunk_id == nchunk - 1)
        shift = (1, S - 1)
        for e in range(2):
            for n, ln in enumerate(lanes):
                h0 = h0_ref[e][:, ln]
                cin = jnp.where(edge[e], h0, 0.0)
                for _ in range(nchunk - 1):
                    cin = jnp.where(edge[e], h0, pltpu.roll(h[e * nlt + n] + p[e * nlt + n] * cin, shift[e], 0))
                cin_s[e, n] = cin

    def emit(c, carry):
        r0 = pl.multiple_of(c * RB, RB)
        s = r0 // Tc
        t0 = r0 % Tc
        src = pl.ds(t0 * S + s, RB, stride=S)
        r = pl.ds(r0, RB)
        for n, ln in enumerate(lanes):
            y = []
            for e in range(2):
                ye = u_s[e, n, src, :]
                if chunked:
                    ye = ye + a_s[e, n, src, :] * cin_s[e, n, pl.ds(s, 1), :]
                y.append(ye)
            yg_ref[r, ln] = ((y[0] + y[1]) * jax.nn.gelu(gt_ref[r, ln])).astype(_BF)
        return carry

    lax.fori_loop(0, R // RB, emit, 0)


def _rg_core(cfg, proj, row_off, S, Tc, nchunk, h0, rgw, j, want_final):
    conv_w, conv_b, w_a, b_a, w_x, b_x, lam = rgw
    rnn = proj.shape[1] // 2
    C = cfg.rg_lanes
    blk = rnn // cfg.rg_blocks
    assert blk == _LANES, "one gate block per lane tile"
    nct = rnn // C
    R = S * Tc
    RB = min(cfg.rg_rows, Tc)
    rb0 = row_off // R
    nrg = conv_w.shape[0]
    vec = lambda arr: arr.reshape(nrg, 2, 1, rnn)
    vspec = pl.BlockSpec((None, 2, 1, C), lambda ct: (j, 0, 0, ct))
    wspec = pl.BlockSpec((None, 2, C // blk, blk, blk), lambda ct: (j, 0, ct, 0, 0))
    out_specs = [pl.BlockSpec((R, C), lambda ct: (0, ct))]
    out_shape = [jax.ShapeDtypeStruct((R, rnn), _BF)]
    if want_final:
        out_specs.append(pl.BlockSpec((2, S, C), lambda ct: (0, 0, ct)))
        out_shape.append(jax.ShapeDtypeStruct((2, S, rnn), _F32))
    return pl.pallas_call(
        functools.partial(_rg_kernel, S=S, Tc=Tc, nchunk=nchunk, RB=RB, want_final=want_final),
        grid=(nct,),
        in_specs=[pl.BlockSpec((R, C), lambda ct: (rb0, nct + ct)),
                  pl.BlockSpec((R, C), lambda ct: (rb0, ct)),
                  pl.BlockSpec((None, _CONV_W, C), lambda ct: (j, 0, ct)),
                  pl.BlockSpec((None, 1, C), lambda ct: (j, 0, ct)),
                  wspec, vspec, wspec, vspec, vspec,
                  pl.BlockSpec((2, S, C), lambda ct: (0, 0, ct))],
        out_specs=out_specs,
        out_shape=out_shape,
        scratch_shapes=[pltpu.VMEM((R + (R // (Tc * nchunk)) * 2 * _SUBLANES, C), _F32),
                        pltpu.VMEM((2, C // _LANES, R, _LANES), _F32),
                        pltpu.VMEM((2, C // _LANES, R, _LANES), _F32),
                        pltpu.VMEM((2, C // _LANES, S, _LANES), _F32)],
        compiler_params=_params(("arbitrary",), 56),
        name="rg_core_ctx" if want_final else "rg_core_lat",
    )(proj, proj, conv_w, conv_b.reshape(nrg, 1, rnn), w_a, vec(b_a), w_x, vec(b_x), vec(lam), h0)


def _attn_kernel(*refs, G, hd, scale, has_cache):
    if has_cache:
        q_ref, k_ref, v_ref, ck_ref, cv_ref, o_ref, k_s, v_s, kc_s, vc_s = refs
    else:
        q_ref, k_ref, v_ref, o_ref, k_s, v_s = refs

    @pl.when(pl.program_id(2) == 0)
    def _cast_keys():
        k_s[...] = k_ref[...].astype(_BF)
        v_s[...] = v_ref[...].astype(_BF)
        if has_cache:
            kc_s[...] = ck_ref[...].astype(_BF)
            vc_s[...] = cv_ref[...].astype(_BF)

    k = k_s[...]
    v = v_s[...]
    if has_cache:
        kc = kc_s[...]
        vc = vc_s[...]
    nt = (((1,), (1,)), ((), ()))
    for g in range(G):
        q = q_ref[:, g * hd:(g + 1) * hd].astype(_BF)
        s1 = lax.dot_general(q, k, nt, preferred_element_type=_F32) * scale
        m = jnp.max(s1, axis=-1, keepdims=True)
        if has_cache:
            s2 = lax.dot_general(q, kc, nt, preferred_element_type=_F32) * scale
            m = jnp.maximum(m, jnp.max(s2, axis=-1, keepdims=True))
        p1 = jnp.exp(s1 - m)
        l = jnp.sum(p1, axis=-1, keepdims=True)
        o = jnp.dot(p1.astype(_BF), v, preferred_element_type=_F32)
        if has_cache:
            p2 = jnp.exp(s2 - m)
            l = l + jnp.sum(p2, axis=-1, keepdims=True)
            o = o + jnp.dot(p2.astype(_BF), vc, preferred_element_type=_F32)
        o_ref[:, g * hd:(g + 1) * hd] = (o / l).astype(_BF)


def _attention(cfg, qkv, row_off, B, T, tq, cache, j, name):
    hd = cfg.d_model // cfg.n_heads
    KV = cfg.n_kv_heads
    G = cfg.n_heads // KV
    nq = T // tq
    qb0 = row_off // tq
    kb0 = row_off // T
    kcol = cfg.n_heads
    vcol = cfg.n_heads + KV
    in_specs = [pl.BlockSpec((tq, G * hd), lambda b, h, qi: (qb0 + b * nq + qi, h)),
                pl.BlockSpec((T, hd), lambda b, h, qi: (kb0 + b, kcol + h)),
                pl.BlockSpec((T, hd), lambda b, h, qi: (kb0 + b, vcol + h))]
    args = [qkv, qkv, qkv]
    scratch = [pltpu.VMEM((T, hd), _BF), pltpu.VMEM((T, hd), _BF)]
    if cache is not None:
        ck, cv = cache
        P = ck.shape[2]
        cspec = pl.BlockSpec((None, None, P, hd), lambda b, h, qi: (b, j, 0, h))
        in_specs += [cspec, cspec]
        args += [ck, cv]
        scratch += [pltpu.VMEM((P, hd), _BF), pltpu.VMEM((P, hd), _BF)]
    return pl.pallas_call(
        functools.partial(_attn_kernel, G=G, hd=hd, scale=1.0 / math.sqrt(hd), has_cache=cache is not None),
        grid=(B, KV, nq),
        in_specs=in_specs,
        out_specs=pl.BlockSpec((tq, G * hd), lambda b, h, qi: (b * nq + qi, h)),
        out_shape=jax.ShapeDtypeStruct((B * T, cfg.n_heads * hd), _BF),
        scratch_shapes=scratch,
        compiler_params=_params(("arbitrary", "arbitrary", "arbitrary"), 48),
        name=name,
    )(*args)


def _forward(cfg, x_prompt, x_sample, state_rglru, cache_k, cache_v, c, c_ctx,
             w_mod, b_mod, g_pre_mix, g_post_mix, g_pre_ffn, g_post_ffn,
             w_qkv, g_q, g_k, w_o,
             w_rg_in, rg_conv_w, rg_conv_b, w_rg_a, b_rg_a, w_rg_x, b_rg_x, rg_lambda, w_rg_out,
             w_ff1, w_ff2):
    D = cfg.d_model
    L = cfg.depth
    B, T = cfg.batch, cfg.seq
    DB, DT = cfg.dec_batch, cfg.dec_seq
    NP = B * T
    hd = D // cfg.n_heads
    KV = cfg.n_kv_heads
    nch = cfg.dec_chunks

    cvec = jnp.zeros((_SUBLANES, D), _F32).at[0].set(c_ctx).at[1:1 + DB].set(c)
    mod = _modulation(cfg, cvec, w_mod, b_mod)[:, :1 + DB].reshape(L, 1 + DB, 6, 1, D)

    gains = lambda g: g.reshape(g.shape[0], 1, g.shape[1])
    g_pre_mix, g_post_mix, g_pre_ffn, g_post_ffn = map(gains, (g_pre_mix, g_post_mix, g_pre_ffn, g_post_ffn))
    g_q, g_k = gains(g_q), gains(g_k)
    rope = _rope_tables(cfg)
    ck = cache_k.reshape(DB, -1, cfg.past_len, KV * hd)
    cv = cache_v.reshape(DB, -1, cfg.past_len, KV * hd)

    x, h = _prenorm(cfg, x_prompt.reshape(NP, D), x_sample.reshape(DB * DT, D), g_pre_mix, mod, 0)
    new_states, new_ks, new_vs = [], [], []
    for l in range(L):
        j = l // 2
        if l % 2 == 0:
            proj = _wres_call(_inproj_kernel, cfg, h, w_rg_in, j, cfg.tn_in, _F32, "rg_in_proj")
            rgw = (rg_conv_w, rg_conv_b, w_rg_a, b_rg_a, w_rg_x, b_rg_x, rg_lambda)
            h0p = jnp.zeros((2, B, D), _F32)
            yp, st = _rg_core(cfg, proj, 0, B, T, 1, h0p, rgw, j, True)
            h0s = jnp.repeat(jnp.swapaxes(state_rglru[:, j], 0, 1), nch, axis=1)
            (ys,) = _rg_core(cfg, proj, NP, DB * nch, DT // nch, nch, h0s, rgw, j, False)
            mix = (yp, ys)
            new_states.append(jnp.swapaxes(st, 0, 1))
            w_out, wl = w_rg_out, j
        else:
            qkv = _qkv(cfg, h, w_qkv, g_q, g_k, rope, j)
            kcols = slice(cfg.n_heads * hd, (cfg.n_heads + KV) * hd)
            vcols = slice((cfg.n_heads + KV) * hd, (cfg.n_heads + 2 * KV) * hd)
            new_ks.append(qkv[:NP, kcols].reshape(B, T, KV, hd))
            new_vs.append(qkv[:NP, vcols].reshape(B, T, KV, hd))
            op = _attention(cfg, qkv, 0, B, T, T, None, j, "attn_ctx")
            os_ = _attention(cfg, qkv, NP, DB, DT, cfg.tq, (ck, cv), j, "attn_lat")
            mix = (op, os_)
            w_out, wl = w_o, j
        x, f = _epi(cfg, mix, w_out, wl, x, mod, l, 2, g_post_mix,
                    (g_pre_ffn, l, l, 3, 4), "mix_out_proj")
        hdn = _wres_call(_ffn1_kernel, cfg, f, w_ff1, l, cfg.tn_ff, _BF, "ffn_up")
        nxt = (g_pre_mix, l + 1, l + 1, 0, 1) if l + 1 < L else None
        x, h = _epi(cfg, (hdn,), w_ff2, l, x, mod, l, 5, g_post_ffn, nxt, "ffn_down")

    return (x[:NP].reshape(B, T, D), x[NP:].reshape(DB, DT, D),
            jnp.stack(new_states, axis=1), jnp.stack(new_ks, axis=1), jnp.stack(new_vs, axis=1))


def kernel(x_prompt, x_sample, state_rglru, cache_k, cache_v, c, c_ctx, w_mod, b_mod, g_pre_mix, g_post_mix, g_pre_ffn, g_post_ffn, w_qkv, g_q, g_k, w_o, w_rg_in, rg_conv_w, rg_conv_b, w_rg_a, b_rg_a, w_rg_x, b_rg_x, rg_lambda, w_rg_out, w_ff1, w_ff2):
    return _forward(_CFG, x_prompt, x_sample, state_rglru, cache_k, cache_v, c, c_ctx,
                    w_mod, b_mod, g_pre_mix, g_post_mix, g_pre_ffn, g_post_ffn,
                    w_qkv, g_q, g_k, w_o,
                    w_rg_in, rg_conv_w, rg_conv_b, w_rg_a, b_rg_a, w_rg_x, b_rg_x, rg_lambda, w_rg_out,
                    w_ff1, w_ff2)
```

```python
import functools
import math
from typing import NamedTuple

import jax
import jax.numpy as jnp
from jax import lax
from jax.experimental import pallas as pl
from jax.experimental.pallas import tpu as pltpu

_BF = jnp.bfloat16
_F32 = jnp.float32

_EPS = 1e-6
_RG_C = 8.0
_ROPE_THETA = 10000.0
_LOG2E = 1.4426950408889634
_SCAN_STEPS = 4
_CONV_W = 4
_LANES = 128
_SUBLANES = 8
_MIB = 1024 * 1024


class _Cfg(NamedTuple):
    d_model: int
    batch: int
    seq: int
    depth: int
    dec_batch: int
    dec_seq: int
    past_len: int
    grid_w: int
    n_heads: int
    n_kv_heads: int
    rg_blocks: int
    d_ff: int
    dec_chunks: int
    tm: int
    tm_mm: int
    tn_in: int
    tn_qkv: int
    tn_ff: int
    tm_epi: int
    tk: int
    n_epi: int
    tn_mod: int
    tq: int
    rg_rows: int
    rg_lanes: int
    epi_rows: int


_CFG = _Cfg(d_model=2048, batch=16, seq=256, depth=4, dec_batch=2, dec_seq=2048, past_len=512,
            grid_w=64, n_heads=16, n_kv_heads=4, rg_blocks=16, d_ff=8192, dec_chunks=4,
            tm=512, tm_mm=1024, tn_in=1024, tn_qkv=512, tn_ff=1024, tm_epi=2048, tk=512, n_epi=8, tn_mod=1024, tq=256,
            rg_rows=256, rg_lanes=256, epi_rows=64)


def _params(sem, vmem_mib):
    return pltpu.CompilerParams(dimension_semantics=sem, vmem_limit_bytes=vmem_mib * _MIB)


def _rms(x, g):
    return x * lax.rsqrt(jnp.mean(x * x, axis=-1, keepdims=True) + _EPS) * g


def _sigmoid(z):
    return 0.5 * jnp.tanh(0.5 * z) + 0.5


def _mod_row(cfg, tm):
    npb = cfg.batch * cfg.seq // tm
    bps = cfg.dec_seq // tm
    return lambda i: jnp.where(i < npb, 0, 1 + (i - npb) // bps)


def _mod_kernel(c_ref, w_ref, b_ref, o_ref):
    s = jax.nn.silu(c_ref[...]).astype(_BF)
    o_ref[...] = jnp.dot(s, w_ref[...].astype(_BF), preferred_element_type=_F32) + b_ref[...]


def _modulation(cfg, cvec, w_mod, b_mod):
    L, D, N = w_mod.shape
    tn = cfg.tn_mod
    return pl.pallas_call(
        _mod_kernel,
        grid=(L, N // tn),
        in_specs=[pl.BlockSpec((_SUBLANES, D), lambda l, j: (0, 0)),
                  pl.BlockSpec((None, D, tn), lambda l, j: (l, 0, j)),
                  pl.BlockSpec((None, 1, tn), lambda l, j: (l, 0, j))],
        out_specs=pl.BlockSpec((None, _SUBLANES, tn), lambda l, j: (l, 0, j)),
        out_shape=jax.ShapeDtypeStruct((L, _SUBLANES, N), _F32),
        compiler_params=_params(("arbitrary", "arbitrary"), 40),
        name="modulation",
    )(cvec, w_mod, b_mod.reshape(L, 1, N))


def _prenorm_kernel(xp_ref, xs_ref, g_ref, sh_ref, sc_ref, x_ref, h_ref, *, n_prompt_blocks):
    def emit(src_ref):
        x = src_ref[...]
        x_ref[...] = x
        h_ref[...] = (_rms(x, g_ref[...]) * (1.0 + sc_ref[...]) + sh_ref[...]).astype(_BF)

    i = pl.program_id(0)
    pl.when(i < n_prompt_blocks)(lambda: emit(xp_ref))
    pl.when(i >= n_prompt_blocks)(lambda: emit(xs_ref))


def _prenorm(cfg, xp, xs, g, mod, l):
    D = xp.shape[1]
    M = xp.shape[0] + xs.shape[0]
    tm = cfg.tm
    npb = xp.shape[0] // tm
    row = _mod_row(cfg, tm)
    mspec = lambda idx: pl.BlockSpec((None, None, None, 1, D), lambda i: (l, row(i), idx, 0, 0))
    out = pl.BlockSpec((tm, D), lambda i: (i, 0))
    return pl.pallas_call(
        functools.partial(_prenorm_kernel, n_prompt_blocks=npb),
        grid=(M // tm,),
        in_specs=[pl.BlockSpec((tm, D), lambda i: (jnp.minimum(i, npb - 1), 0)),
                  pl.BlockSpec((tm, D), lambda i: (jnp.maximum(i - npb, 0), 0)),
                  pl.BlockSpec((None, 1, D), lambda i: (l, 0, 0)),
                  mspec(0), mspec(1)],
        out_specs=[out, out],
        out_shape=[jax.ShapeDtypeStruct((M, D), _F32), jax.ShapeDtypeStruct((M, D), _BF)],
        compiler_params=_params(("arbitrary",), 32),
        name="prenorm",
    )(xp, xs, g, mod, mod)


def _wres_dot(a_ref, w_ref, wbf_ref):
    @pl.when(pl.program_id(1) == 0)
    def _cast():
        wbf_ref[...] = w_ref[...].astype(_BF)
    return jnp.dot(a_ref[...], wbf_ref[...], preferred_element_type=_F32)


def _inproj_kernel(a_ref, w_ref, o_ref, wbf_ref):
    o_ref[...] = _wres_dot(a_ref, w_ref, wbf_ref)


def _ffn1_kernel(a_ref, w_ref, o_ref, wbf_ref):
    z = jnp.maximum(_wres_dot(a_ref, w_ref, wbf_ref), 0.0)
    o_ref[...] = (z * z).astype(_BF)


def _wres_call(kernel, cfg, a, w, l, tn, out_dtype, name, extra_in=(), extra_specs=()):
    M, K = a.shape
    N = w.shape[-1]
    tm = cfg.tm_mm
    return pl.pallas_call(
        kernel,
        grid=(N // tn, M // tm),
        in_specs=[pl.BlockSpec((tm, K), lambda j, i: (i, 0)),
                  pl.BlockSpec((None, K, tn), lambda j, i: (l, 0, j)),
                  *extra_specs],
        out_specs=pl.BlockSpec((tm, tn), lambda j, i: (i, j)),
        out_shape=jax.ShapeDtypeStruct((M, N), out_dtype),
        scratch_shapes=[pltpu.VMEM((K, tn), _BF)],
        compiler_params=_params(("arbitrary", "arbitrary"), 48),
        name=name,
    )(a, w, *extra_in)


def _rope(z, cos, sin, first_half):
    hd = z.shape[-1]
    quarter = hd // 4
    partner = jnp.where(first_half, pltpu.roll(z, hd - quarter, 1), pltpu.roll(z, quarter, 1))
    return z * cos + partner * sin


def _qkv_kernel(a_ref, w_ref, gq_ref, gk_ref, cos_ref, sin_ref, o_ref, wbf_ref, *,
                n_q_tiles, n_prompt_blocks, hd):
    j = pl.program_id(0)
    i = pl.program_id(1)
    acc = _wres_dot(a_ref, w_ref, wbf_ref)
    heads = acc.shape[-1] // hd
    normed = j <= n_q_tiles
    g = jnp.where(j < n_q_tiles, gq_ref[...], gk_ref[...])

    @pl.when(jnp.logical_not(normed))
    def _v():
        o_ref[...] = acc

    @pl.when(jnp.logical_and(normed, i < n_prompt_blocks))
    def _context():
        for h in range(heads):
            o_ref[:, h * hd:(h + 1) * hd] = _rms(acc[:, h * hd:(h + 1) * hd], g)

    @pl.when(jnp.logical_and(normed, i >= n_prompt_blocks))
    def _latent():
        cos = cos_ref[...]
        sin = sin_ref[...]
        lane = lax.broadcasted_iota(jnp.int32, cos.shape, 1)
        first_half = (lane % (hd // 2)) < (hd // 4)
        for h in range(heads):
            z = _rms(acc[:, h * hd:(h + 1) * hd], g)
            o_ref[:, h * hd:(h + 1) * hd] = _rope(z, cos, sin, first_half)


def _rope_tables(cfg):
    hd = cfg.d_model // cfg.n_heads
    pairs = hd // 4
    n = cfg.dec_seq
    rows = n // cfg.grid_w
    row = jnp.broadcast_to(jnp.arange(rows, dtype=_F32)[:, None], (rows, cfg.grid_w)).reshape(-1)
    col = jnp.broadcast_to(jnp.arange(cfg.grid_w, dtype=_F32)[None, :], (rows, cfg.grid_w)).reshape(-1)
    inv_freq = _ROPE_THETA ** (-jnp.arange(pairs, dtype=_F32) / pairs)
    ang_r = row[:, None] * inv_freq[None, :]
    ang_c = col[:, None] * inv_freq[None, :]
    cos = jnp.concatenate([jnp.cos(ang_r), jnp.cos(ang_r), jnp.cos(ang_c), jnp.cos(ang_c)], axis=-1)
    sin = jnp.concatenate([-jnp.sin(ang_r), jnp.sin(ang_r), -jnp.sin(ang_c), jnp.sin(ang_c)], axis=-1)
    return cos, sin


def _qkv(cfg, h, w_qkv, g_q, g_k, rope, j):
    hd = cfg.d_model // cfg.n_heads
    tm, tn = cfg.tm_mm, cfg.tn_qkv
    npb = cfg.batch * cfg.seq // tm
    bps = cfg.dec_seq // tm
    assert (cfg.n_kv_heads * hd) == tn, "one column tile must hold exactly the k heads"
    tab = pl.BlockSpec((tm, hd), lambda jj, i: (jnp.maximum(i - npb, 0) % bps, 0))
    gain = pl.BlockSpec((None, 1, hd), lambda jj, i: (j, 0, 0))
    kern = functools.partial(_qkv_kernel, n_q_tiles=cfg.n_heads * hd // tn, n_prompt_blocks=npb, hd=hd)
    return _wres_call(kern, cfg, h, w_qkv, j, tn, _F32, "qkv_proj",
                      extra_in=(g_q, g_k, *rope), extra_specs=(gain, gain, tab, tab))


def _epi_kernel(*refs, n_a, split, nk, te, with_next, rows):
    a_refs, refs = refs[:n_a], refs[n_a:]
    if with_next:
        w_ref, x_ref, gate_ref, gpost_ref, gnext_ref, sh_ref, sc_ref, xo_ref, ho_ref, acc_ref = refs
    else:
        w_ref, x_ref, gate_ref, gpost_ref, xo_ref, acc_ref = refs
    i = pl.program_id(0)
    k = pl.program_id(1)

    @pl.when(k == 0)
    def _zero():
        acc_ref[...] = jnp.zeros(acc_ref.shape, _F32)

    def accumulate(a_ref):
        acc_ref[...] += jnp.dot(a_ref[...], w_ref[...].astype(_BF), preferred_element_type=_F32)

    if n_a == 1:
        pl.when(k < nk)(lambda: accumulate(a_refs[0]))
    else:
        pl.when(jnp.logical_and(k < nk, i < split))(lambda: accumulate(a_refs[0]))
        pl.when(jnp.logical_and(k < nk, i >= split))(lambda: accumulate(a_refs[1]))

    @pl.when(k >= nk)
    def _epilogue():
        base = (k - nk) * te
        gate = gate_ref[...]
        gpost = gpost_ref[...]

        def chunk(c, carry):
            r = pl.ds(pl.multiple_of(c * rows, rows), rows)
            ra = pl.ds(pl.multiple_of(base + c * rows, rows), rows)
            xn = x_ref[r, :] + gate * _rms(acc_ref[ra, :], gpost)
            xo_ref[r, :] = xn
            if with_next:
                ho_ref[r, :] = (_rms(xn, gnext_ref[...]) * (1.0 + sc_ref[...]) + sh_ref[...]).astype(_BF)
            return carry

        lax.fori_loop(0, te // rows, chunk, 0)


def _epi(cfg, a_parts, w, wl, x, mod, l, gate_idx, gpost, nxt, name):
    K = a_parts[0].shape[1]
    M = sum(p.shape[0] for p in a_parts)
    D = w.shape[-1]
    tm, tk, ne = cfg.tm_epi, cfg.tk, cfg.n_epi
    te = tm // ne
    nk = K // tk
    ni = M // tm
    row = _mod_row(cfg, tm)
    ai = lambda i, k: jnp.where(k < nk, i, jnp.minimum(i + 1, ni - 1))
    ak = lambda i, k: jnp.where(k < nk, k, jnp.where(i + 1 < ni, 0, nk - 1))
    mspec = lambda ml, idx: pl.BlockSpec((None, None, None, 1, D), lambda i, k: (ml, row(i), idx, 0, 0))
    gspec = lambda gl: pl.BlockSpec((None, 1, D), lambda i, k: (gl, 0, 0))
    xspec = pl.BlockSpec((te, D), lambda i, k: (i * ne + jnp.clip(k - nk, 0, ne - 1), 0))
    n1 = a_parts[0].shape[0] // tm
    if len(a_parts) == 1:
        a_specs = [pl.BlockSpec((tm, tk), lambda i, k: (ai(i, k), ak(i, k)))]
    else:
        first = lambda i, k: ai(i, k) < n1
        a_specs = [pl.BlockSpec((tm, tk), lambda i, k: (jnp.where(first(i, k), ai(i, k), n1 - 1),
                                                        jnp.where(first(i, k), ak(i, k), nk - 1))),
                   pl.BlockSpec((tm, tk), lambda i, k: (jnp.where(first(i, k), 0, ai(i, k) - n1),
                                                        jnp.where(first(i, k), 0, ak(i, k))))]
    in_specs = [*a_specs,
                pl.BlockSpec((None, tk, D), lambda i, k: (wl, ak(i, k), 0)),
                xspec, mspec(l, gate_idx), gspec(l)]
    args = [*a_parts, w, x, mod, gpost]
    out_specs = [xspec]
    out_shape = [jax.ShapeDtypeStruct((M, D), _F32)]
    if nxt is not None:
        gnext, gl, ml, sh_idx, sc_idx = nxt
        in_specs += [gspec(gl), mspec(ml, sh_idx), mspec(ml, sc_idx)]
        args += [gnext, mod, mod]
        out_specs.append(xspec)
        out_shape.append(jax.ShapeDtypeStruct((M, D), _BF))
    out = pl.pallas_call(
        functools.partial(_epi_kernel, n_a=len(a_parts), split=n1, nk=nk, te=te,
                          with_next=nxt is not None, rows=cfg.epi_rows),
        grid=(ni, nk + ne),
        in_specs=in_specs,
        out_specs=out_specs,
        out_shape=out_shape,
        scratch_shapes=[pltpu.VMEM((tm, D), _F32)],
        compiler_params=_params(("arbitrary", "arbitrary"), 56),
        name=name,
    )(*args)
    return out if nxt is not None else (out[0], None)


def _rg_kernel(*refs, S, Tc, nchunk, RB, want_final):
    (xb_ref, gt_ref, cw_ref, cb_ref, wa_ref, ba_ref, wx_ref, bx_ref, lam_ref, h0_ref), rest = refs[:10], refs[10:]
    if want_final:
        yg_ref, hfin_ref, xp_s, a_s, u_s, cin_s = rest
    else:
        yg_ref, xp_s, a_s, u_s, cin_s = rest
    R = S * Tc
    T = Tc * nchunk
    C = xb_ref.shape[-1]
    nlt = C // _LANES
    pad = _SUBLANES
    lanes = [slice(n * _LANES, (n + 1) * _LANES) for n in range(nlt)]

    nseq = R // T
    span = T + 2 * pad

    def fill(q, carry):
        base = pl.multiple_of(q * span, pad)
        xp_s[pl.ds(base, pad), :] = jnp.zeros((pad, C), _F32)
        xp_s[pl.ds(base + pad, T), :] = xb_ref[pl.ds(pl.multiple_of(q * T, T), T), :]
        xp_s[pl.ds(base + pad + T, pad), :] = jnp.zeros((pad, C), _F32)
        return carry

    lax.fori_loop(0, nseq, fill, 0)

    wa = (0.5 * wa_ref[...]).astype(_BF)
    wx = (0.5 * wx_ref[...]).astype(_BF)
    ba = 0.5 * ba_ref[...]
    bx = 0.5 * bx_ref[...]
    nceh = (0.5 * _RG_C) * jax.nn.softplus(-lam_ref[...])
    cw = cw_ref[...]
    cb = cb_ref[...]

    def gates(c, carry):
        r0 = pl.multiple_of(c * RB, RB)
        nwin = RB + 2 * pad
        s = r0 // Tc
        t0 = r0 % Tc
        dst = pl.ds(t0 * S + s, RB, stride=S)
        w0 = pl.multiple_of(r0 + (r0 // T) * 2 * pad, pad)
        for n, ln in enumerate(lanes):
            win = xp_s[pl.ds(w0, nwin), ln]
            x0 = win[pad:pad + RB]
            xm1 = pltpu.roll(win, 1, 0)[pad:pad + RB]
            xp1 = pltpu.roll(win, nwin - 1, 0)[pad:pad + RB]
            xp2 = pltpu.roll(win, nwin - 2, 0)[pad:pad + RB]
            xc = cb[:, ln] + xm1 * cw[0:1, ln] + x0 * cw[1:2, ln] + xp1 * cw[2:3, ln] + xp2 * cw[3:4, ln]
            xh = 0.5 * xc
            x16 = xc.astype(_BF)
            for e in range(2):
                tr = jnp.tanh(jnp.dot(x16, wa[e, n], preferred_element_type=_F32) + ba[e][:, ln])
                ti = jnp.tanh(jnp.dot(x16, wx[e, n], preferred_element_type=_F32) + bx[e][:, ln])
                nla = nceh[e][:, ln] * tr + nceh[e][:, ln]
                a = jnp.exp2(nla * (-_LOG2E))
                a_s[e, n, dst, :] = a
                w = jnp.tanh(nla) * (a * a + 1.0)
                sqrt_w = jnp.where(w > 0.0, w * lax.rsqrt(w), 0.0)
                u_s[e, n, dst, :] = sqrt_w * (ti * xh + xh)
        return carry

    lax.fori_loop(0, R // RB, gates, 0)

    chunked = nchunk > 1

    chains = [(e, n) for e in range(2) for n in range(nlt)]
    steps = _SCAN_STEPS

    def scan_block(tb, carry):
        h, p = list(carry[0]), list(carry[1])
        rows = [(pl.ds(pl.multiple_of((tb * steps + k) * S, S), S),
                 pl.ds(pl.multiple_of((Tc - 1 - tb * steps - k) * S, S), S)) for k in range(steps)]
        av = [[a_s[e, n, rows[k][e], :] for k in range(steps)] for e, n in chains]
        uv = [[u_s[e, n, rows[k][e], :] for k in range(steps)] for e, n in chains]
        hs, ps = [], []
        for k in range(steps):
            for i in range(len(chains)):
                h[i] = av[i][k] * h[i] + uv[i][k]
                hs.append(h[i])
                if chunked:
                    p[i] = av[i][k] * p[i]
                    ps.append(p[i])
        for k in range(steps):
            for i, (e, n) in enumerate(chains):
                u_s[e, n, rows[k][e], :] = hs[k * len(chains) + i]
                if chunked:
                    a_s[e, n, rows[k][e], :] = ps[k * len(chains) + i]
        return tuple(h), tuple(p)

    zeros = jnp.zeros((S, _LANES), _F32)
    ones = jnp.ones((S, _LANES), _F32)
    if chunked:
        init = ((zeros,) * (2 * nlt), (ones,) * (2 * nlt))
    else:
        init = (tuple(h0_ref[e][:, ln] for e in range(2) for ln in lanes), ())
    h, p = lax.fori_loop(0, Tc // steps, scan_block, init)

    if want_final:
        for e in range(2):
            for n, ln in enumerate(lanes):
                hfin_ref[e, :, ln] = h[e * nlt + n]

    if chunked:
        chunk_id = lax.broadcasted_iota(jnp.int32, (S, _LANES), 0) % nchunk
        edge = (chunk_id == 0, chunk_id == nchunk - 1)
        shift = (1, S - 1)
        for e in range(2):
            for n, ln in enumerate(lanes):
                h0 = h0_ref[e][:, ln]
                cin = jnp.where(edge[e], h0, 0.0)
                for _ in range(nchunk - 1):
                    cin = jnp.where(edge[e], h0, pltpu.roll(h[e * nlt + n] + p[e * nlt + n] * cin, shift[e], 0))
                cin_s[e, n] = cin

    def emit(c, carry):
        r0 = pl.multiple_of(c * RB, RB)
        s = r0 // Tc
        t0 = r0 % Tc
        src = pl.ds(t0 * S + s, RB, stride=S)
        r = pl.ds(r0, RB)
        for n, ln in enumerate(lanes):
            y = []
            for e in range(2):
                ye = u_s[e, n, src, :]
                if chunked:
                    ye = ye + a_s[e, n, src, :] * cin_s[e, n, pl.ds(s, 1), :]
                y.append(ye)
            yg_ref[r, ln] = ((y[0] + y[1]) * jax.nn.gelu(gt_ref[r, ln])).astype(_BF)
        return carry

    lax.fori_loop(0, R // RB, emit, 0)


def _rg_core(cfg, proj, row_off, S, Tc, nchunk, h0, rgw, j, want_final):
    conv_w, conv_b, w_a, b_a, w_x, b_x, lam = rgw
    rnn = proj.shape[1] // 2
    C = cfg.rg_lanes
    blk = rnn // cfg.rg_blocks
    assert blk == _LANES, "one gate block per lane tile"
    nct = rnn // C
    R = S * Tc
    RB = min(cfg.rg_rows, Tc)
    rb0 = row_off // R
    nrg = conv_w.shape[0]
    vec = lambda arr: arr.reshape(nrg, 2, 1, rnn)
    vspec = pl.BlockSpec((None, 2, 1, C), lambda ct: (j, 0, 0, ct))
    wspec = pl.BlockSpec((None, 2, C // blk, blk, blk), lambda ct: (j, 0, ct, 0, 0))
    out_specs = [pl.BlockSpec((R, C), lambda ct: (0, ct))]
    out_shape = [jax.ShapeDtypeStruct((R, rnn), _BF)]
    if want_final:
        out_specs.append(pl.BlockSpec((2, S, C), lambda ct: (0, 0, ct)))
        out_shape.append(jax.ShapeDtypeStruct((2, S, rnn), _F32))
    return pl.pallas_call(
        functools.partial(_rg_kernel, S=S, Tc=Tc, nchunk=nchunk, RB=RB, want_final=want_final),
        grid=(nct,),
        in_specs=[pl.BlockSpec((R, C), lambda ct: (rb0, nct + ct)),
                  pl.BlockSpec((R, C), lambda ct: (rb0, ct)),
                  pl.BlockSpec((None, _CONV_W, C), lambda ct: (j, 0, ct)),
                  pl.BlockSpec((None, 1, C), lambda ct: (j, 0, ct)),
                  wspec, vspec, wspec, vspec, vspec,
                  pl.BlockSpec((2, S, C), lambda ct: (0, 0, ct))],
        out_specs=out_specs,
        out_shape=out_shape,
        scratch_shapes=[pltpu.VMEM((R + (R // (Tc * nchunk)) * 2 * _SUBLANES, C), _F32),
                        pltpu.VMEM((2, C // _LANES, R, _LANES), _F32),
                        pltpu.VMEM((2, C // _LANES, R, _LANES), _F32),
                        pltpu.VMEM((2, C // _LANES, S, _LANES), _F32)],
        compiler_params=_params(("arbitrary",), 56),
        name="rg_core_ctx" if want_final else "rg_core_lat",
    )(proj, proj, conv_w, conv_b.reshape(nrg, 1, rnn), w_a, vec(b_a), w_x, vec(b_x), vec(lam), h0)


def _attn_kernel(*refs, G, hd, scale, has_cache):
    if has_cache:
        q_ref, k_ref, v_ref, ck_ref, cv_ref, o_ref, k_s, v_s, kc_s, vc_s = refs
    else:
        q_ref, k_ref, v_ref, o_ref, k_s, v_s = refs

    @pl.when(pl.program_id(2) == 0)
    def _cast_keys():
        k_s[...] = k_ref[...].astype(_BF)
        v_s[...] = v_ref[...].astype(_BF)
        if has_cache:
            kc_s[...] = ck_ref[...].astype(_BF)
            vc_s[...] = cv_ref[...].astype(_BF)

    k = k_s[...]
    v = v_s[...]
    if has_cache:
        kc = kc_s[...]
        vc = vc_s[...]
    nt = (((1,), (1,)), ((), ()))
    for g in range(G):
        q = q_ref[:, g * hd:(g + 1) * hd].astype(_BF)
        s1 = lax.dot_general(q, k, nt, preferred_element_type=_F32) * scale
        m = jnp.max(s1, axis=-1, keepdims=True)
        if has_cache:
            s2 = lax.dot_general(q, kc, nt, preferred_element_type=_F32) * scale
            m = jnp.maximum(m, jnp.max(s2, axis=-1, keepdims=True))
        p1 = jnp.exp(s1 - m)
        l = jnp.sum(p1, axis=-1, keepdims=True)
        o = jnp.dot(p1.astype(_BF), v, preferred_element_type=_F32)
        if has_cache:
            p2 = jnp.exp(s2 - m)
            l = l + jnp.sum(p2, axis=-1, keepdims=True)
            o = o + jnp.dot(p2.astype(_BF), vc, preferred_element_type=_F32)
        o_ref[:, g * hd:(g + 1) * hd] = (o / l).astype(_BF)


def _attention(cfg, qkv, row_off, B, T, tq, cache, j, name):
    hd = cfg.d_model // cfg.n_heads
    KV = cfg.n_kv_heads
    G = cfg.n_heads // KV
    nq = T // tq
    qb0 = row_off // tq
    kb0 = row_off // T
    kcol = cfg.n_heads
    vcol = cfg.n_heads + KV
    in_specs = [pl.BlockSpec((tq, G * hd), lambda b, h, qi: (qb0 + b * nq + qi, h)),
                pl.BlockSpec((T, hd), lambda b, h, qi: (kb0 + b, kcol + h)),
                pl.BlockSpec((T, hd), lambda b, h, qi: (kb0 + b, vcol + h))]
    args = [qkv, qkv, qkv]
    scratch = [pltpu.VMEM((T, hd), _BF), pltpu.VMEM((T, hd), _BF)]
    if cache is not None:
        ck, cv = cache
        P = ck.shape[2]
        cspec = pl.BlockSpec((None, None, P, hd), lambda b, h, qi: (b, j, 0, h))
        in_specs += [cspec, cspec]
        args += [ck, cv]
        scratch += [pltpu.VMEM((P, hd), _BF), pltpu.VMEM((P, hd), _BF)]
    return pl.pallas_call(
        functools.partial(_attn_kernel, G=G, hd=hd, scale=1.0 / math.sqrt(hd), has_cache=cache is not None),
        grid=(B, KV, nq),
        in_specs=in_specs,
        out_specs=pl.BlockSpec((tq, G * hd), lambda b, h, qi: (b * nq + qi, h)),
        out_shape=jax.ShapeDtypeStruct((B * T, cfg.n_heads * hd), _BF),
        scratch_shapes=scratch,
        compiler_params=_params(("arbitrary", "arbitrary", "arbitrary"), 48),
        name=name,
    )(*args)


def _forward(cfg, x_prompt, x_sample, state_rglru, cache_k, cache_v, c, c_ctx,
             w_mod, b_mod, g_pre_mix, g_post_mix, g_pre_ffn, g_post_ffn,
             w_qkv, g_q, g_k, w_o,
             w_rg_in, rg_conv_w, rg_conv_b, w_rg_a, b_rg_a, w_rg_x, b_rg_x, rg_lambda, w_rg_out,
             w_ff1, w_ff2):
    D = cfg.d_model
    L = cfg.depth
    B, T = cfg.batch, cfg.seq
    DB, DT = cfg.dec_batch, cfg.dec_seq
    NP = B * T
    hd = D // cfg.n_heads
    KV = cfg.n_kv_heads
    nch = cfg.dec_chunks

    cvec = jnp.zeros((_SUBLANES, D), _F32).at[0].set(c_ctx).at[1:1 + DB].set(c)
    mod = _modulation(cfg, cvec, w_mod, b_mod)[:, :1 + DB].reshape(L, 1 + DB, 6, 1, D)

    gains = lambda g: g.reshape(g.shape[0], 1, g.shape[1])
    g_pre_mix, g_post_mix, g_pre_ffn, g_post_ffn = map(gains, (g_pre_mix, g_post_mix, g_pre_ffn, g_post_ffn))
    g_q, g_k = gains(g_q), gains(g_k)
    rope = _rope_tables(cfg)
    ck = cache_k.reshape(DB, -1, cfg.past_len, KV * hd)
    cv = cache_v.reshape(DB, -1, cfg.past_len, KV * hd)

    x, h = _prenorm(cfg, x_prompt.reshape(NP, D), x_sample.reshape(DB * DT, D), g_pre_mix, mod, 0)
    new_states, new_ks, new_vs = [], [], []
    for l in range(L):
        j = l // 2
        if l % 2 == 0:
            proj = _wres_call(_inproj_kernel, cfg, h, w_rg_in, j, cfg.tn_in, _F32, "rg_in_proj")
            rgw = (rg_conv_w, rg_conv_b, w_rg_a, b_rg_a, w_rg_x, b_rg_x, rg_lambda)
            h0p = jnp.zeros((2, B, D), _F32)
            yp, st = _rg_core(cfg, proj, 0, B, T, 1, h0p, rgw, j, True)
            h0s = jnp.repeat(jnp.swapaxes(state_rglru[:, j], 0, 1), nch, axis=1)
            (ys,) = _rg_core(cfg, proj, NP, DB * nch, DT // nch, nch, h0s, rgw, j, False)
            mix = (yp, ys)
            new_states.append(jnp.swapaxes(st, 0, 1))
            w_out, wl = w_rg_out, j
        else:
            qkv = _qkv(cfg, h, w_qkv, g_q, g_k, rope, j)
            kcols = slice(cfg.n_heads * hd, (cfg.n_heads + KV) * hd)
            vcols = slice((cfg.n_heads + KV) * hd, (cfg.n_heads + 2 * KV) * hd)
            new_ks.append(qkv[:NP, kcols].reshape(B, T, KV, hd))
            new_vs.append(qkv[:NP, vcols].reshape(B, T, KV, hd))
            op = _attention(cfg, qkv, 0, B, T, T, None, j, "attn_ctx")
            os_ = _attention(cfg, qkv, NP, DB, DT, cfg.tq, (ck, cv), j, "attn_lat")
            mix = (op, os_)
            w_out, wl = w_o, j
        x, f = _epi(cfg, mix, w_out, wl, x, mod, l, 2, g_post_mix,
                    (g_pre_ffn, l, l, 3, 4), "mix_out_proj")
        hdn = _wres_call(_ffn1_kernel, cfg, f, w_ff1, l, cfg.tn_ff, _BF, "ffn_up")
        nxt = (g_pre_mix, l + 1, l + 1, 0, 1) if l + 1 < L else None
        x, h = _epi(cfg, (hdn,), w_ff2, l, x, mod, l, 5, g_post_ffn, nxt, "ffn_down")

    return (x[:NP].reshape(B, T, D), x[NP:].reshape(DB, DT, D),
            jnp.stack(new_states, axis=1), jnp.stack(new_ks, axis=1), jnp.stack(new_vs, axis=1))


def kernel(x_prompt, x_sample, state_rglru, cache_k, cache_v, c, c_ctx, w_mod, b_mod, g_pre_mix, g_post_mix, g_pre_ffn, g_post_ffn, w_qkv, g_q, g_k, w_o, w_rg_in, rg_conv_w, rg_conv_b, w_rg_a, b_rg_a, w_rg_x, b_rg_x, rg_lambda, w_rg_out, w_ff1, w_ff2):
    return _forward(_CFG, x_prompt, x_sample, state_rglru, cache_k, cache_v, c, c_ctx,
                    w_mod, b_mod, g_pre_mix, g_post_mix, g_pre_ffn, g_post_ffn,
                    w_qkv, g_q, g_k, w_o,
                    w_rg_in, rg_conv_w, rg_conv_b, w_rg_a, b_rg_a, w_rg_x, b_rg_x, rg_lambda, w_rg_out,
                    w_ff1, w_ff2)
```

```python
import functools
import math
from typing import NamedTuple

import jax
import jax.numpy as jnp
from jax import lax
from jax.experimental import pallas as pl
from jax.experimental.pallas import tpu as pltpu

_BF = jnp.bfloat16
_F32 = jnp.float32

_EPS = 1e-6
_RG_C = 8.0
_ROPE_THETA = 10000.0
_LOG2E = 1.4426950408889634
_SCAN_STEPS = 4
_CONV_W = 4
_LANES = 128
_SUBLANES = 8
_MIB = 1024 * 1024


class _Cfg(NamedTuple):
    d_model: int
    batch: int
    seq: int
    depth: int
    dec_batch: int
    dec_seq: int
    past_len: int
    grid_w: int
    n_heads: int
    n_kv_heads: int
    rg_blocks: int
    d_ff: int
    dec_chunks: int
    tm: int
    tm_mm: int
    tn_in: int
    tn_qkv: int
    tn_ff: int
    tm_epi: int
    tk: int
    n_epi: int
    tn_mod: int
    tq: int
    rg_rows: int
    rg_lanes: int
    epi_rows: int


_CFG = _Cfg(d_model=2048, batch=16, seq=256, depth=4, dec_batch=2, dec_seq=2048, past_len=512,
            grid_w=64, n_heads=16, n_kv_heads=4, rg_blocks=16, d_ff=8192, dec_chunks=4,
            tm=512, tm_mm=1024, tn_in=1024, tn_qkv=512, tn_ff=1024, tm_epi=2048, tk=512, n_epi=8, tn_mod=1024, tq=256,
            rg_rows=256, rg_lanes=256, epi_rows=64)


def _params(sem, vmem_mib):
    return pltpu.CompilerParams(dimension_semantics=sem, vmem_limit_bytes=vmem_mib * _MIB)


def _rms(x, g):
    return x * lax.rsqrt(jnp.mean(x * x, axis=-1, keepdims=True) + _EPS) * g


def _sigmoid(z):
    return 0.5 * jnp.tanh(0.5 * z) + 0.5


def _mod_row(cfg, tm):
    npb = cfg.batch * cfg.seq // tm
    bps = cfg.dec_seq // tm
    return lambda i: jnp.where(i < npb, 0, 1 + (i - npb) // bps)


def _mod_kernel(c_ref, w_ref, b_ref, o_ref):
    s = jax.nn.silu(c_ref[...]).astype(_BF)
    o_ref[...] = jnp.dot(s, w_ref[...].astype(_BF), preferred_element_type=_F32) + b_ref[...]


def _modulation(cfg, cvec, w_mod, b_mod):
    L, D, N = w_mod.shape
    tn = cfg.tn_mod
    return pl.pallas_call(
        _mod_kernel,
        grid=(L, N // tn),
        in_specs=[pl.BlockSpec((_SUBLANES, D), lambda l, j: (0, 0)),
                  pl.BlockSpec((None, D, tn), lambda l, j: (l, 0, j)),
                  pl.BlockSpec((None, 1, tn), lambda l, j: (l, 0, j))],
        out_specs=pl.BlockSpec((None, _SUBLANES, tn), lambda l, j: (l, 0, j)),
        out_shape=jax.ShapeDtypeStruct((L, _SUBLANES, N), _F32),
        compiler_params=_params(("arbitrary", "arbitrary"), 40),
        name="modulation",
    )(cvec, w_mod, b_mod.reshape(L, 1, N))


def _prenorm_kernel(xp_ref, xs_ref, g_ref, sh_ref, sc_ref, x_ref, h_ref, *, n_prompt_blocks):
    def emit(src_ref):
        x = src_ref[...]
        x_ref[...] = x
        h_ref[...] = (_rms(x, g_ref[...]) * (1.0 + sc_ref[...]) + sh_ref[...]).astype(_BF)

    i = pl.program_id(0)
    pl.when(i < n_prompt_blocks)(lambda: emit(xp_ref))
    pl.when(i >= n_prompt_blocks)(lambda: emit(xs_ref))


def _prenorm(cfg, xp, xs, g, mod, l):
    D = xp.shape[1]
    M = xp.shape[0] + xs.shape[0]
    tm = cfg.tm
    npb = xp.shape[0] // tm
    row = _mod_row(cfg, tm)
    mspec = lambda idx: pl.BlockSpec((None, None, None, 1, D), lambda i: (l, row(i), idx, 0, 0))
    out = pl.BlockSpec((tm, D), lambda i: (i, 0))
    return pl.pallas_call(
        functools.partial(_prenorm_kernel, n_prompt_blocks=npb),
        grid=(M // tm,),
        in_specs=[pl.BlockSpec((tm, D), lambda i: (jnp.minimum(i, npb - 1), 0)),
                  pl.BlockSpec((tm, D), lambda i: (jnp.maximum(i - npb, 0), 0)),
                  pl.BlockSpec((None, 1, D), lambda i: (l, 0, 0)),
                  mspec(0), mspec(1)],
        out_specs=[out, out],
        out_shape=[jax.ShapeDtypeStruct((M, D), _F32), jax.ShapeDtypeStruct((M, D), _BF)],
        compiler_params=_params(("arbitrary",), 32),
        name="prenorm",
    )(xp, xs, g, mod, mod)


def _wres_dot(a_ref, w_ref, wbf_ref):
    @pl.when(pl.program_id(1) == 0)
    def _cast():
        wbf_ref[...] = w_ref[...].astype(_BF)
    return jnp.dot(a_ref[...], wbf_ref[...], preferred_element_type=_F32)


def _inproj_kernel(a_ref, w_ref, o_ref, wbf_ref):
    o_ref[...] = _wres_dot(a_ref, w_ref, wbf_ref)


def _ffn1_kernel(a_ref, w_ref, o_ref, wbf_ref):
    z = jnp.maximum(_wres_dot(a_ref, w_ref, wbf_ref), 0.0)
    o_ref[...] = (z * z).astype(_BF)


def _wres_call(kernel, cfg, a, w, l, tn, out_dtype, name, extra_in=(), extra_specs=()):
    M, K = a.shape
    N = w.shape[-1]
    tm = cfg.tm_mm
    return pl.pallas_call(
        kernel,
        grid=(N // tn, M // tm),
        in_specs=[pl.BlockSpec((tm, K), lambda j, i: (i, 0)),
                  pl.BlockSpec((None, K, tn), lambda j, i: (l, 0, j)),
                  *extra_specs],
        out_specs=pl.BlockSpec((tm, tn), lambda j, i: (i, j)),
        out_shape=jax.ShapeDtypeStruct((M, N), out_dtype),
        scratch_shapes=[pltpu.VMEM((K, tn), _BF)],
        compiler_params=_params(("arbitrary", "arbitrary"), 48),
        name=name,
    )(a, w, *extra_in)


def _rope(z, cos, sin, first_half):
    hd = z.shape[-1]
    quarter = hd // 4
    partner = jnp.where(first_half, pltpu.roll(z, hd - quarter, 1), pltpu.roll(z, quarter, 1))
    return z * cos + partner * sin


def _qkv_kernel(a_ref, w_ref, gq_ref, gk_ref, cos_ref, sin_ref, o_ref, wbf_ref, *,
                n_q_tiles, n_prompt_blocks, hd):
    j = pl.program_id(0)
    i = pl.program_id(1)
    acc = _wres_dot(a_ref, w_ref, wbf_ref)
    heads = acc.shape[-1] // hd
    normed = j <= n_q_tiles
    g = jnp.where(j < n_q_tiles, gq_ref[...], gk_ref[...])

    @pl.when(jnp.logical_not(normed))
    def _v():
        o_ref[...] = acc

    @pl.when(jnp.logical_and(normed, i < n_prompt_blocks))
    def _context():
        for h in range(heads):
            o_ref[:, h * hd:(h + 1) * hd] = _rms(acc[:, h * hd:(h + 1) * hd], g)

    @pl.when(jnp.logical_and(normed, i >= n_prompt_blocks))
    def _latent():
        cos = cos_ref[...]
        sin = sin_ref[...]
        lane = lax.broadcasted_iota(jnp.int32, cos.shape, 1)
        first_half = (lane % (hd // 2)) < (hd // 4)
        for h in range(heads):
            z = _rms(acc[:, h * hd:(h + 1) * hd], g)
            o_ref[:, h * hd:(h + 1) * hd] = _rope(z, cos, sin, first_half)


def _rope_tables(cfg):
    hd = cfg.d_model // cfg.n_heads
    pairs = hd // 4
    n = cfg.dec_seq
    rows = n // cfg.grid_w
    row = jnp.broadcast_to(jnp.arange(rows, dtype=_F32)[:, None], (rows, cfg.grid_w)).reshape(-1)
    col = jnp.broadcast_to(jnp.arange(cfg.grid_w, dtype=_F32)[None, :], (rows, cfg.grid_w)).reshape(-1)
    inv_freq = _ROPE_THETA ** (-jnp.arange(pairs, dtype=_F32) / pairs)
    ang_r = row[:, None] * inv_freq[None, :]
    ang_c = col[:, None] * inv_freq[None, :]
    cos = jnp.concatenate([jnp.cos(ang_r), jnp.cos(ang_r), jnp.cos(ang_c), jnp.cos(ang_c)], axis=-1)
    sin = jnp.concatenate([-jnp.sin(ang_r), jnp.sin(ang_r), -jnp.sin(ang_c), jnp.sin(ang_c)], axis=-1)
    return cos, sin


def _qkv(cfg, h, w_qkv, g_q, g_k, rope, j):
    hd = cfg.d_model // cfg.n_heads
    tm, tn = cfg.tm_mm, cfg.tn_qkv
    npb = cfg.batch * cfg.seq // tm
    bps = cfg.dec_seq // tm
    assert (cfg.n_kv_heads * hd) == tn, "one column tile must hold exactly the k heads"
    tab = pl.BlockSpec((tm, hd), lambda jj, i: (jnp.maximum(i - npb, 0) % bps, 0))
    gain = pl.BlockSpec((None, 1, hd), lambda jj, i: (j, 0, 0))
    kern = functools.partial(_qkv_kernel, n_q_tiles=cfg.n_heads * hd // tn, n_prompt_blocks=npb, hd=hd)
    return _wres_call(kern, cfg, h, w_qkv, j, tn, _F32, "qkv_proj",
                      extra_in=(g_q, g_k, *rope), extra_specs=(gain, gain, tab, tab))


def _epi_kernel(*refs, n_a, split, nk, te, with_next, rows):
    a_refs, refs = refs[:n_a], refs[n_a:]
    if with_next:
        w_ref, x_ref, gate_ref, gpost_ref, gnext_ref, sh_ref, sc_ref, xo_ref, ho_ref, acc_ref = refs
    else:
        w_ref, x_ref, gate_ref, gpost_ref, xo_ref, acc_ref = refs
    i = pl.program_id(0)
    k = pl.program_id(1)

    @pl.when(k == 0)
    def _zero():
        acc_ref[...] = jnp.zeros(acc_ref.shape, _F32)

    def accumulate(a_ref):
        acc_ref[...] += jnp.dot(a_ref[...], w_ref[...].astype(_BF), preferred_element_type=_F32)

    if n_a == 1:
        pl.when(k < nk)(lambda: accumulate(a_refs[0]))
    else:
        pl.when(jnp.logical_and(k < nk, i < split))(lambda: accumulate(a_refs[0]))
        pl.when(jnp.logical_and(k < nk, i >= split))(lambda: accumulate(a_refs[1]))

    @pl.when(k >= nk)
    def _epilogue():
        base = (k - nk) * te
        gain = gate_ref[...] * gpost_ref[...]
        if with_next:
            gain_next = gnext_ref[...] * (1.0 + sc_ref[...])
        for c in range(te // rows):
            r = slice(c * rows, (c + 1) * rows)
            ra = pl.ds(pl.multiple_of(base + c * rows, rows), rows)
            xn = x_ref[r, :] + _rms(acc_ref[ra, :], gain)
            xo_ref[r, :] = xn
            if with_next:
                ho_ref[r, :] = (_rms(xn, gain_next) + sh_ref[...]).astype(_BF)


def _epi(cfg, a_parts, w, wl, x, mod, l, gate_idx, gpost, nxt, name):
    K = a_parts[0].shape[1]
    M = sum(p.shape[0] for p in a_parts)
    D = w.shape[-1]
    tm, tk, ne = cfg.tm_epi, cfg.tk, cfg.n_epi
    te = tm // ne
    nk = K // tk
    ni = M // tm
    row = _mod_row(cfg, tm)
    ai = lambda i, k: jnp.where(k < nk, i, jnp.minimum(i + 1, ni - 1))
    ak = lambda i, k: jnp.where(k < nk, k, jnp.where(i + 1 < ni, 0, nk - 1))
    mspec = lambda ml, idx: pl.BlockSpec((None, None, None, 1, D), lambda i, k: (ml, row(i), idx, 0, 0))
    gspec = lambda gl: pl.BlockSpec((None, 1, D), lambda i, k: (gl, 0, 0))
    xspec = pl.BlockSpec((te, D), lambda i, k: (i * ne + jnp.clip(k - nk, 0, ne - 1), 0))
    n1 = a_parts[0].shape[0] // tm
    if len(a_parts) == 1:
        a_specs = [pl.BlockSpec((tm, tk), lambda i, k: (ai(i, k), ak(i, k)))]
    else:
        first = lambda i, k: ai(i, k) < n1
        a_specs = [pl.BlockSpec((tm, tk), lambda i, k: (jnp.where(first(i, k), ai(i, k), n1 - 1),
                                                        jnp.where(first(i, k), ak(i, k), nk - 1))),
                   pl.BlockSpec((tm, tk), lambda i, k: (jnp.where(first(i, k), 0, ai(i, k) - n1),
                                                        jnp.where(first(i, k), 0, ak(i, k))))]
    in_specs = [*a_specs,
                pl.BlockSpec((None, tk, D), lambda i, k: (wl, ak(i, k), 0)),
                xspec, mspec(l, gate_idx), gspec(l)]
    args = [*a_parts, w, x, mod, gpost]
    out_specs = [xspec]
    out_shape = [jax.ShapeDtypeStruct((M, D), _F32)]
    if nxt is not None:
        gnext, gl, ml, sh_idx, sc_idx = nxt
        in_specs += [gspec(gl), mspec(ml, sh_idx), mspec(ml, sc_idx)]
        args += [gnext, mod, mod]
        out_specs.append(xspec)
        out_shape.append(jax.ShapeDtypeStruct((M, D), _BF))
    out = pl.pallas_call(
        functools.partial(_epi_kernel, n_a=len(a_parts), split=n1, nk=nk, te=te,
                          with_next=nxt is not None, rows=cfg.epi_rows),
        grid=(ni, nk + ne),
        in_specs=in_specs,
        out_specs=out_specs,
        out_shape=out_shape,
        scratch_shapes=[pltpu.VMEM((tm, D), _F32)],
        compiler_params=_params(("arbitrary", "arbitrary"), 56),
        name=name,
    )(*args)
    return out if nxt is not None else (out[0], None)


def _rg_kernel(*refs, S, Tc, nchunk, RB, want_final):
    (xb_ref, gt_ref, cw_ref, cb_ref, wa_ref, ba_ref, wx_ref, bx_ref, lam_ref, h0_ref), rest = refs[:10], refs[10:]
    if want_final:
        yg_ref, hfin_ref, xp_s, a_s, u_s, cin_s = rest
    else:
        yg_ref, xp_s, a_s, u_s, cin_s = rest
    R = S * Tc
    T = Tc * nchunk
    C = xb_ref.shape[-1]
    nlt = C // _LANES
    pad = _SUBLANES
    lanes = [slice(n * _LANES, (n + 1) * _LANES) for n in range(nlt)]

    nseq = R // T
    span = T + 2 * pad

    def fill(q, carry):
        base = pl.multiple_of(q * span, pad)
        xp_s[pl.ds(base, pad), :] = jnp.zeros((pad, C), _F32)
        xp_s[pl.ds(base + pad, T), :] = xb_ref[pl.ds(pl.multiple_of(q * T, T), T), :]
        xp_s[pl.ds(base + pad + T, pad), :] = jnp.zeros((pad, C), _F32)
        return carry

    lax.fori_loop(0, nseq, fill, 0)

    wa = (0.5 * wa_ref[...]).astype(_BF)
    wx = (0.5 * wx_ref[...]).astype(_BF)
    ba = 0.5 * ba_ref[...]
    bx = 0.5 * bx_ref[...]
    nceh = (0.5 * _RG_C) * jax.nn.softplus(-lam_ref[...])
    cw = cw_ref[...]
    cb = cb_ref[...]

    def gates(c, carry):
        r0 = pl.multiple_of(c * RB, RB)
        nwin = RB + 2 * pad
        s = r0 // Tc
        t0 = r0 % Tc
        dst = pl.ds(t0 * S + s, RB, stride=S)
        w0 = pl.multiple_of(r0 + (r0 // T) * 2 * pad, pad)
        for n, ln in enumerate(lanes):
            win = xp_s[pl.ds(w0, nwin), ln]
            x0 = win[pad:pad + RB]
            xm1 = pltpu.roll(win, 1, 0)[pad:pad + RB]
            xp1 = pltpu.roll(win, nwin - 1, 0)[pad:pad + RB]
            xp2 = pltpu.roll(win, nwin - 2, 0)[pad:pad + RB]
            xc = cb[:, ln] + xm1 * cw[0:1, ln] + x0 * cw[1:2, ln] + xp1 * cw[2:3, ln] + xp2 * cw[3:4, ln]
            xh = 0.5 * xc
            x16 = xc.astype(_BF)
            for e in range(2):
                tr = jnp.tanh(jnp.dot(x16, wa[e, n], preferred_element_type=_F32) + ba[e][:, ln])
                ti = jnp.tanh(jnp.dot(x16, wx[e, n], preferred_element_type=_F32) + bx[e][:, ln])
                nla = nceh[e][:, ln] * tr + nceh[e][:, ln]
                a = jnp.exp2(nla * (-_LOG2E))
                a_s[e, n, dst, :] = a
                w = jnp.tanh(nla) * (a * a + 1.0)
                sqrt_w = jnp.where(w > 0.0, w * lax.rsqrt(w), 0.0)
                u_s[e, n, dst, :] = sqrt_w * (ti * xh + xh)
        return carry

    lax.fori_loop(0, R // RB, gates, 0)

    chunked = nchunk > 1

    chains = [(e, n) for e in range(2) for n in range(nlt)]
    steps = _SCAN_STEPS

    def scan_block(tb, carry):
        h, p = list(carry[0]), list(carry[1])
        rows = [(pl.ds(pl.multiple_of((tb * steps + k) * S, S), S),
                 pl.ds(pl.multiple_of((Tc - 1 - tb * steps - k) * S, S), S)) for k in range(steps)]
        av = [[a_s[e, n, rows[k][e], :] for k in range(steps)] for e, n in chains]
        uv = [[u_s[e, n, rows[k][e], :] for k in range(steps)] for e, n in chains]
        hs, ps = [], []
        for k in range(steps):
            for i in range(len(chains)):
                h[i] = av[i][k] * h[i] + uv[i][k]
                hs.append(h[i])
                if chunked:
                    p[i] = av[i][k] * p[i]
                    ps.append(p[i])
        for k in range(steps):
            for i, (e, n) in enumerate(chains):
                u_s[e, n, rows[k][e], :] = hs[k * len(chains) + i]
                if chunked:
                    a_s[e, n, rows[k][e], :] = ps[k * len(chains) + i]
        return tuple(h), tuple(p)

    zeros = jnp.zeros((S, _LANES), _F32)
    ones = jnp.ones((S, _LANES), _F32)
    if chunked:
        init = ((zeros,) * (2 * nlt), (ones,) * (2 * nlt))
    else:
        init = (tuple(h0_ref[e][:, ln] for e in range(2) for ln in lanes), ())
    h, p = lax.fori_loop(0, Tc // steps, scan_block, init)

    if want_final:
        for e in range(2):
            for n, ln in enumerate(lanes):
                hfin_ref[e, :, ln] = h[e * nlt + n]

    if chunked:
        chunk_id = lax.broadcasted_iota(jnp.int32, (S, _LANES), 0) % nchunk
        edge = (chunk_id == 0, chunk_id == nchunk - 1)
        shift = (1, S - 1)
        for e in range(2):
            for n, ln in enumerate(lanes):
                h0 = h0_ref[e][:, ln]
                cin = jnp.where(edge[e], h0, 0.0)
                for _ in range(nchunk - 1):
                    cin = jnp.where(edge[e], h0, pltpu.roll(h[e * nlt + n] + p[e * nlt + n] * cin, shift[e], 0))
                cin_s[e, n] = cin

    def emit(c, carry):
        r0 = pl.multiple_of(c * RB, RB)
        s = r0 // Tc
        t0 = r0 % Tc
        src = pl.ds(t0 * S + s, RB, stride=S)
        r = pl.ds(r0, RB)
        for n, ln in enumerate(lanes):
            y = []
            for e in range(2):
                ye = u_s[e, n, src, :]
                if chunked:
                    ye = ye + a_s[e, n, src, :] * cin_s[e, n, pl.ds(s, 1), :]
                y.append(ye)
            yg_ref[r, ln] = ((y[0] + y[1]) * jax.nn.gelu(gt_ref[r, ln])).astype(_BF)
        return carry

    lax.fori_loop(0, R // RB, emit, 0)


def _rg_core(cfg, proj, row_off, S, Tc, nchunk, h0, rgw, j, want_final):
    conv_w, conv_b, w_a, b_a, w_x, b_x, lam = rgw
    rnn = proj.shape[1] // 2
    C = cfg.rg_lanes
    blk = rnn // cfg.rg_blocks
    assert blk == _LANES, "one gate block per lane tile"
    nct = rnn // C
    R = S * Tc
    RB = min(cfg.rg_rows, Tc)
    rb0 = row_off // R
    nrg = conv_w.shape[0]
    vec = lambda arr: arr.reshape(nrg, 2, 1, rnn)
    vspec = pl.BlockSpec((None, 2, 1, C), lambda ct: (j, 0, 0, ct))
    wspec = pl.BlockSpec((None, 2, C // blk, blk, blk), lambda ct: (j, 0, ct, 0, 0))
    out_specs = [pl.BlockSpec((R, C), lambda ct: (0, ct))]
    out_shape = [jax.ShapeDtypeStruct((R, rnn), _BF)]
    if want_final:
        out_specs.append(pl.BlockSpec((2, S, C), lambda ct: (0, 0, ct)))
        out_shape.append(jax.ShapeDtypeStruct((2, S, rnn), _F32))
    return pl.pallas_call(
        functools.partial(_rg_kernel, S=S, Tc=Tc, nchunk=nchunk, RB=RB, want_final=want_final),
        grid=(nct,),
        in_specs=[pl.BlockSpec((R, C), lambda ct: (rb0, nct + ct)),
                  pl.BlockSpec((R, C), lambda ct: (rb0, ct)),
                  pl.BlockSpec((None, _CONV_W, C), lambda ct: (j, 0, ct)),
                  pl.BlockSpec((None, 1, C), lambda ct: (j, 0, ct)),
                  wspec, vspec, wspec, vspec, vspec,
                  pl.BlockSpec((2, S, C), lambda ct: (0, 0, ct))],
        out_specs=out_specs,
        out_shape=out_shape,
        scratch_shapes=[pltpu.VMEM((R + (R // (Tc * nchunk)) * 2 * _SUBLANES, C), _F32),
                        pltpu.VMEM((2, C // _LANES, R, _LANES), _F32),
                        pltpu.VMEM((2, C // _LANES, R, _LANES), _F32),
                        pltpu.VMEM((2, C // _LANES, S, _LANES), _F32)],
        compiler_params=_params(("arbitrary",), 56),
        name="rg_core_ctx" if want_final else "rg_core_lat",
    )(proj, proj, conv_w, conv_b.reshape(nrg, 1, rnn), w_a, vec(b_a), w_x, vec(b_x), vec(lam), h0)


def _attn_kernel(*refs, G, hd, scale, has_cache):
    if has_cache:
        q_ref, k_ref, v_ref, ck_ref, cv_ref, o_ref, k_s, v_s, kc_s, vc_s = refs
    else:
        q_ref, k_ref, v_ref, o_ref, k_s, v_s = refs

    @pl.when(pl.program_id(2) == 0)
    def _cast_keys():
        k_s[...] = k_ref[...].astype(_BF)
        v_s[...] = v_ref[...].astype(_BF)
        if has_cache:
            kc_s[...] = ck_ref[...].astype(_BF)
            vc_s[...] = cv_ref[...].astype(_BF)

    k = k_s[...]
    v = v_s[...]
    if has_cache:
        kc = kc_s[...]
        vc = vc_s[...]
    nt = (((1,), (1,)), ((), ()))
    for g in range(G):
        q = (q_ref[:, g * hd:(g + 1) * hd] * scale).astype(_BF)
        s1 = lax.dot_general(q, k, nt, preferred_element_type=_F32)
        m = jnp.max(s1, axis=-1, keepdims=True)
        if has_cache:
            s2 = lax.dot_general(q, kc, nt, preferred_element_type=_F32)
            m = jnp.maximum(m, jnp.max(s2, axis=-1, keepdims=True))
        p1 = jnp.exp(s1 - m)
        l = jnp.sum(p1, axis=-1, keepdims=True)
        o = jnp.dot(p1.astype(_BF), v, preferred_element_type=_F32)
        if has_cache:
            p2 = jnp.exp(s2 - m)
            l = l + jnp.sum(p2, axis=-1, keepdims=True)
            o = o + jnp.dot(p2.astype(_BF), vc, preferred_element_type=_F32)
        o_ref[:, g * hd:(g + 1) * hd] = (o / l).astype(_BF)


def _attention(cfg, qkv, row_off, B, T, tq, cache, j, name):
    hd = cfg.d_model // cfg.n_heads
    KV = cfg.n_kv_heads
    G = cfg.n_heads // KV
    nq = T // tq
    qb0 = row_off // tq
    kb0 = row_off // T
    kcol = cfg.n_heads
    vcol = cfg.n_heads + KV
    in_specs = [pl.BlockSpec((tq, G * hd), lambda b, h, qi: (qb0 + b * nq + qi, h)),
                pl.BlockSpec((T, hd), lambda b, h, qi: (kb0 + b, kcol + h)),
                pl.BlockSpec((T, hd), lambda b, h, qi: (kb0 + b, vcol + h))]
    args = [qkv, qkv, qkv]
    scratch = [pltpu.VMEM((T, hd), _BF), pltpu.VMEM((T, hd), _BF)]
    if cache is not None:
        ck, cv = cache
        P = ck.shape[2]
        cspec = pl.BlockSpec((None, None, P, hd), lambda b, h, qi: (b, j, 0, h))
        in_specs += [cspec, cspec]
        args += [ck, cv]
        scratch += [pltpu.VMEM((P, hd), _BF), pltpu.VMEM((P, hd), _BF)]
    return pl.pallas_call(
        functools.partial(_attn_kernel, G=G, hd=hd, scale=1.0 / math.sqrt(hd), has_cache=cache is not None),
        grid=(B, KV, nq),
        in_specs=in_specs,
        out_specs=pl.BlockSpec((tq, G * hd), lambda b, h, qi: (b * nq + qi, h)),
        out_shape=jax.ShapeDtypeStruct((B * T, cfg.n_heads * hd), _BF),
        scratch_shapes=scratch,
        compiler_params=_params(("arbitrary", "arbitrary", "arbitrary"), 48),
        name=name,
    )(*args)


def _forward(cfg, x_prompt, x_sample, state_rglru, cache_k, cache_v, c, c_ctx,
             w_mod, b_mod, g_pre_mix, g_post_mix, g_pre_ffn, g_post_ffn,
             w_qkv, g_q, g_k, w_o,
             w_rg_in, rg_conv_w, rg_conv_b, w_rg_a, b_rg_a, w_rg_x, b_rg_x, rg_lambda, w_rg_out,
             w_ff1, w_ff2):
    D = cfg.d_model
    L = cfg.depth
    B, T = cfg.batch, cfg.seq
    DB, DT = cfg.dec_batch, cfg.dec_seq
    NP = B * T
    hd = D // cfg.n_heads
    KV = cfg.n_kv_heads
    nch = cfg.dec_chunks

    cvec = jnp.zeros((_SUBLANES, D), _F32).at[0].set(c_ctx).at[1:1 + DB].set(c)
    mod = _modulation(cfg, cvec, w_mod, b_mod)[:, :1 + DB].reshape(L, 1 + DB, 6, 1, D)

    gains = lambda g: g.reshape(g.shape[0], 1, g.shape[1])
    g_pre_mix, g_post_mix, g_pre_ffn, g_post_ffn = map(gains, (g_pre_mix, g_post_mix, g_pre_ffn, g_post_ffn))
    g_q, g_k = gains(g_q), gains(g_k)
    rope = _rope_tables(cfg)
    ck = cache_k.reshape(DB, -1, cfg.past_len, KV * hd)
    cv = cache_v.reshape(DB, -1, cfg.past_len, KV * hd)

    x, h = _prenorm(cfg, x_prompt.reshape(NP, D), x_sample.reshape(DB * DT, D), g_pre_mix, mod, 0)
    new_states, new_ks, new_vs = [], [], []
    for l in range(L):
        j = l // 2
        if l % 2 == 0:
            proj = _wres_call(_inproj_kernel, cfg, h, w_rg_in, j, cfg.tn_in, _F32, "rg_in_proj")
            rgw = (rg_conv_w, rg_conv_b, w_rg_a, b_rg_a, w_rg_x, b_rg_x, rg_lambda)
            h0p = jnp.zeros((2, B, D), _F32)
            yp, st = _rg_core(cfg, proj, 0, B, T, 1, h0p, rgw, j, True)
            h0s = jnp.repeat(jnp.swapaxes(state_rglru[:, j], 0, 1), nch, axis=1)
            (ys,) = _rg_core(cfg, proj, NP, DB * nch, DT // nch, nch, h0s, rgw, j, False)
            mix = (yp, ys)
            new_states.append(jnp.swapaxes(st, 0, 1))
            w_out, wl = w_rg_out, j
        else:
            qkv = _qkv(cfg, h, w_qkv, g_q, g_k, rope, j)
            kcols = slice(cfg.n_heads * hd, (cfg.n_heads + KV) * hd)
            vcols = slice((cfg.n_heads + KV) * hd, (cfg.n_heads + 2 * KV) * hd)
            new_ks.append(qkv[:NP, kcols].reshape(B, T, KV, hd))
            new_vs.append(qkv[:NP, vcols].reshape(B, T, KV, hd))
            op = _attention(cfg, qkv, 0, B, T, T, None, j, "attn_ctx")
            os_ = _attention(cfg, qkv, NP, DB, DT, cfg.tq, (ck, cv), j, "attn_lat")
            mix = (op, os_)
            w_out, wl = w_o, j
        x, f = _epi(cfg, mix, w_out, wl, x, mod, l, 2, g_post_mix,
                    (g_pre_ffn, l, l, 3, 4), "mix_out_proj")
        hdn = _wres_call(_ffn1_kernel, cfg, f, w_ff1, l, cfg.tn_ff, _BF, "ffn_up")
        nxt = (g_pre_mix, l + 1, l + 1, 0, 1) if l + 1 < L else None
        x, h = _epi(cfg, (hdn,), w_ff2, l, x, mod, l, 5, g_post_ffn, nxt, "ffn_down")

    return (x[:NP].reshape(B, T, D), x[NP:].reshape(DB, DT, D),
            jnp.stack(new_states, axis=1), jnp.stack(new_ks, axis=1), jnp.stack(new_vs, axis=1))


def kernel(x_prompt, x_sample, state_rglru, cache_k, cache_v, c, c_ctx, w_mod, b_mod, g_pre_mix, g_post_mix, g_pre_ffn, g_post_ffn, w_qkv, g_q, g_k, w_o, w_rg_in, rg_conv_w, rg_conv_b, w_rg_a, b_rg_a, w_rg_x, b_rg_x, rg_lambda, w_rg_out, w_ff1, w_ff2):
    return _forward(_CFG, x_prompt, x_sample, state_rglru, cache_k, cache_v, c, c_ctx,
                    w_mod, b_mod, g_pre_mix, g_post_mix, g_pre_ffn, g_post_ffn,
                    w_qkv, g_q, g_k, w_o,
                    w_rg_in, rg_conv_w, rg_conv_b, w_rg_a, b_rg_a, w_rg_x, b_rg_x, rg_lambda, w_rg_out,
                    w_ff1, w_ff2)
```

```python
import functools
import math
from typing import NamedTuple

import jax
import jax.numpy as jnp
from jax import lax
from jax.experimental import pallas as pl
from jax.experimental.pallas import tpu as pltpu

_BF = jnp.bfloat16
_F32 = jnp.float32

_EPS = 1e-6
_RG_C = 8.0
_ROPE_THETA = 10000.0
_LOG2E = 1.4426950408889634
_SCAN_STEPS = 4
_CONV_W = 4
_LANES = 128
_SUBLANES = 8
_MIB = 1024 * 1024


class _Cfg(NamedTuple):
    d_model: int
    batch: int
    seq: int
    depth: int
    dec_batch: int
    dec_seq: int
    past_len: int
    grid_w: int
    n_heads: int
    n_kv_heads: int
    rg_blocks: int
    d_ff: int
    dec_chunks: int
    tm: int
    tm_mm: int
    tn_in: int
    tn_qkv: int
    tn_ff: int
    tm_epi: int
    tk: int
    n_epi: int
    tn_mod: int
    tq: int
    rg_rows: int
    rg_lanes: int
    epi_rows: int


_CFG = _Cfg(d_model=2048, batch=16, seq=256, depth=4, dec_batch=2, dec_seq=2048, past_len=512,
            grid_w=64, n_heads=16, n_kv_heads=4, rg_blocks=16, d_ff=8192, dec_chunks=4,
            tm=512, tm_mm=1024, tn_in=1024, tn_qkv=512, tn_ff=1024, tm_epi=2048, tk=512, n_epi=8, tn_mod=1024, tq=512,
            rg_rows=256, rg_lanes=256, epi_rows=64)


def _params(sem, vmem_mib):
    return pltpu.CompilerParams(dimension_semantics=sem, vmem_limit_bytes=vmem_mib * _MIB)


def _rms(x, g):
    return x * lax.rsqrt(jnp.mean(x * x, axis=-1, keepdims=True) + _EPS) * g


def _sigmoid(z):
    return 0.5 * jnp.tanh(0.5 * z) + 0.5


def _mod_row(cfg, tm):
    npb = cfg.batch * cfg.seq // tm
    bps = cfg.dec_seq // tm
    return lambda i: jnp.where(i < npb, 0, 1 + (i - npb) // bps)


def _mod_kernel(c_ref, w_ref, b_ref, o_ref):
    s = jax.nn.silu(c_ref[...]).astype(_BF)
    o_ref[...] = jnp.dot(s, w_ref[...].astype(_BF), preferred_element_type=_F32) + b_ref[...]


def _modulation(cfg, cvec, w_mod, b_mod):
    L, D, N = w_mod.shape
    tn = cfg.tn_mod
    return pl.pallas_call(
        _mod_kernel,
        grid=(L, N // tn),
        in_specs=[pl.BlockSpec((_SUBLANES, D), lambda l, j: (0, 0)),
                  pl.BlockSpec((None, D, tn), lambda l, j: (l, 0, j)),
                  pl.BlockSpec((None, 1, tn), lambda l, j: (l, 0, j))],
        out_specs=pl.BlockSpec((None, _SUBLANES, tn), lambda l, j: (l, 0, j)),
        out_shape=jax.ShapeDtypeStruct((L, _SUBLANES, N), _F32),
        compiler_params=_params(("arbitrary", "arbitrary"), 40),
        name="modulation",
    )(cvec, w_mod, b_mod.reshape(L, 1, N))


def _prenorm_kernel(xp_ref, xs_ref, g_ref, sh_ref, sc_ref, x_ref, h_ref, *, n_prompt_blocks):
    def emit(src_ref):
        x = src_ref[...]
        x_ref[...] = x
        h_ref[...] = (_rms(x, g_ref[...]) * (1.0 + sc_ref[...]) + sh_ref[...]).astype(_BF)

    i = pl.program_id(0)
    pl.when(i < n_prompt_blocks)(lambda: emit(xp_ref))
    pl.when(i >= n_prompt_blocks)(lambda: emit(xs_ref))


def _prenorm(cfg, xp, xs, g, mod, l):
    D = xp.shape[1]
    M = xp.shape[0] + xs.shape[0]
    tm = cfg.tm
    npb = xp.shape[0] // tm
    row = _mod_row(cfg, tm)
    mspec = lambda idx: pl.BlockSpec((None, None, None, 1, D), lambda i: (l, row(i), idx, 0, 0))
    out = pl.BlockSpec((tm, D), lambda i: (i, 0))
    return pl.pallas_call(
        functools.partial(_prenorm_kernel, n_prompt_blocks=npb),
        grid=(M // tm,),
        in_specs=[pl.BlockSpec((tm, D), lambda i: (jnp.minimum(i, npb - 1), 0)),
                  pl.BlockSpec((tm, D), lambda i: (jnp.maximum(i - npb, 0), 0)),
                  pl.BlockSpec((None, 1, D), lambda i: (l, 0, 0)),
                  mspec(0), mspec(1)],
        out_specs=[out, out],
        out_shape=[jax.ShapeDtypeStruct((M, D), _F32), jax.ShapeDtypeStruct((M, D), _BF)],
        compiler_params=_params(("arbitrary",), 32),
        name="prenorm",
    )(xp, xs, g, mod, mod)


def _wres_dot(a_ref, w_ref, wbf_ref):
    @pl.when(pl.program_id(1) == 0)
    def _cast():
        wbf_ref[...] = w_ref[...].astype(_BF)
    return jnp.dot(a_ref[...], wbf_ref[...], preferred_element_type=_F32)


def _inproj_kernel(a_ref, w_ref, o_ref, wbf_ref):
    o_ref[...] = _wres_dot(a_ref, w_ref, wbf_ref)


def _ffn1_kernel(a_ref, w_ref, o_ref, wbf_ref):
    z = jnp.maximum(_wres_dot(a_ref, w_ref, wbf_ref), 0.0)
    o_ref[...] = (z * z).astype(_BF)


def _wres_call(kernel, cfg, a, w, l, tn, out_dtype, name, extra_in=(), extra_specs=()):
    M, K = a.shape
    N = w.shape[-1]
    tm = cfg.tm_mm
    return pl.pallas_call(
        kernel,
        grid=(N // tn, M // tm),
        in_specs=[pl.BlockSpec((tm, K), lambda j, i: (i, 0)),
                  pl.BlockSpec((None, K, tn), lambda j, i: (l, 0, j)),
                  *extra_specs],
        out_specs=pl.BlockSpec((tm, tn), lambda j, i: (i, j)),
        out_shape=jax.ShapeDtypeStruct((M, N), out_dtype),
        scratch_shapes=[pltpu.VMEM((K, tn), _BF)],
        compiler_params=_params(("arbitrary", "arbitrary"), 48),
        name=name,
    )(a, w, *extra_in)


def _rope(z, cos, sin, first_half):
    hd = z.shape[-1]
    quarter = hd // 4
    partner = jnp.where(first_half, pltpu.roll(z, hd - quarter, 1), pltpu.roll(z, quarter, 1))
    return z * cos + partner * sin


def _qkv_kernel(a_ref, w_ref, gq_ref, gk_ref, cos_ref, sin_ref, o_ref, wbf_ref, *,
                n_q_tiles, n_prompt_blocks, hd):
    j = pl.program_id(0)
    i = pl.program_id(1)
    acc = _wres_dot(a_ref, w_ref, wbf_ref)
    heads = acc.shape[-1] // hd
    normed = j <= n_q_tiles
    g = jnp.where(j < n_q_tiles, gq_ref[...], gk_ref[...])

    @pl.when(jnp.logical_not(normed))
    def _v():
        o_ref[...] = acc

    @pl.when(jnp.logical_and(normed, i < n_prompt_blocks))
    def _context():
        for h in range(heads):
            o_ref[:, h * hd:(h + 1) * hd] = _rms(acc[:, h * hd:(h + 1) * hd], g)

    @pl.when(jnp.logical_and(normed, i >= n_prompt_blocks))
    def _latent():
        cos = cos_ref[...]
        sin = sin_ref[...]
        lane = lax.broadcasted_iota(jnp.int32, cos.shape, 1)
        first_half = (lane % (hd // 2)) < (hd // 4)
        for h in range(heads):
            z = _rms(acc[:, h * hd:(h + 1) * hd], g)
            o_ref[:, h * hd:(h + 1) * hd] = _rope(z, cos, sin, first_half)


def _rope_tables(cfg):
    hd = cfg.d_model // cfg.n_heads
    pairs = hd // 4
    n = cfg.dec_seq
    rows = n // cfg.grid_w
    row = jnp.broadcast_to(jnp.arange(rows, dtype=_F32)[:, None], (rows, cfg.grid_w)).reshape(-1)
    col = jnp.broadcast_to(jnp.arange(cfg.grid_w, dtype=_F32)[None, :], (rows, cfg.grid_w)).reshape(-1)
    inv_freq = _ROPE_THETA ** (-jnp.arange(pairs, dtype=_F32) / pairs)
    ang_r = row[:, None] * inv_freq[None, :]
    ang_c = col[:, None] * inv_freq[None, :]
    cos = jnp.concatenate([jnp.cos(ang_r), jnp.cos(ang_r), jnp.cos(ang_c), jnp.cos(ang_c)], axis=-1)
    sin = jnp.concatenate([-jnp.sin(ang_r), jnp.sin(ang_r), -jnp.sin(ang_c), jnp.sin(ang_c)], axis=-1)
    return cos, sin


def _qkv(cfg, h, w_qkv, g_q, g_k, rope, j):
    hd = cfg.d_model // cfg.n_heads
    tm, tn = cfg.tm_mm, cfg.tn_qkv
    npb = cfg.batch * cfg.seq // tm
    bps = cfg.dec_seq // tm
    assert (cfg.n_kv_heads * hd) == tn, "one column tile must hold exactly the k heads"
    tab = pl.BlockSpec((tm, hd), lambda jj, i: (jnp.maximum(i - npb, 0) % bps, 0))
    gain = pl.BlockSpec((None, 1, hd), lambda jj, i: (j, 0, 0))
    kern = functools.partial(_qkv_kernel, n_q_tiles=cfg.n_heads * hd // tn, n_prompt_blocks=npb, hd=hd)
    return _wres_call(kern, cfg, h, w_qkv, j, tn, _F32, "qkv_proj",
                      extra_in=(g_q, g_k, *rope), extra_specs=(gain, gain, tab, tab))


def _epi_kernel(*refs, n_a, split, nk, te, with_next, rows):
    a_refs, refs = refs[:n_a], refs[n_a:]
    if with_next:
        w_ref, x_ref, gate_ref, gpost_ref, gnext_ref, sh_ref, sc_ref, xo_ref, ho_ref, acc_ref = refs
    else:
        w_ref, x_ref, gate_ref, gpost_ref, xo_ref, acc_ref = refs
    i = pl.program_id(0)
    k = pl.program_id(1)

    def accumulate(a_ref, first):
        part = jnp.dot(a_ref[...], w_ref[...].astype(_BF), preferred_element_type=_F32)
        if first:
            acc_ref[...] = part
        else:
            acc_ref[...] += part

    for p, a_ref in enumerate(a_refs):
        mine = True if n_a == 1 else (i < split) == (p == 0)
        pl.when(jnp.logical_and(mine, k == 0))(functools.partial(accumulate, a_ref, True))
        pl.when(jnp.logical_and(mine, jnp.logical_and(k > 0, k < nk)))(functools.partial(accumulate, a_ref, False))

    @pl.when(k >= nk)
    def _epilogue():
        base = (k - nk) * te
        gain = gate_ref[...] * gpost_ref[...]
        if with_next:
            gain_next = gnext_ref[...] * (1.0 + sc_ref[...])
        for c in range(te // rows):
            r = slice(c * rows, (c + 1) * rows)
            ra = pl.ds(pl.multiple_of(base + c * rows, rows), rows)
            xn = x_ref[r, :] + _rms(acc_ref[ra, :], gain)
            xo_ref[r, :] = xn
            if with_next:
                ho_ref[r, :] = (_rms(xn, gain_next) + sh_ref[...]).astype(_BF)


def _epi(cfg, a_parts, w, wl, x, mod, l, gate_idx, gpost, nxt, name):
    K = a_parts[0].shape[1]
    M = sum(p.shape[0] for p in a_parts)
    D = w.shape[-1]
    tm, tk, ne = cfg.tm_epi, cfg.tk, cfg.n_epi
    te = tm // ne
    nk = K // tk
    ni = M // tm
    row = _mod_row(cfg, tm)
    ai = lambda i, k: jnp.where(k < nk, i, jnp.minimum(i + 1, ni - 1))
    ak = lambda i, k: jnp.where(k < nk, k, jnp.where(i + 1 < ni, 0, nk - 1))
    mspec = lambda ml, idx: pl.BlockSpec((None, None, None, 1, D), lambda i, k: (ml, row(i), idx, 0, 0))
    gspec = lambda gl: pl.BlockSpec((None, 1, D), lambda i, k: (gl, 0, 0))
    xspec = pl.BlockSpec((te, D), lambda i, k: (i * ne + jnp.clip(k - nk, 0, ne - 1), 0))
    n1 = a_parts[0].shape[0] // tm
    if len(a_parts) == 1:
        a_specs = [pl.BlockSpec((tm, tk), lambda i, k: (ai(i, k), ak(i, k)))]
    else:
        first = lambda i, k: ai(i, k) < n1
        a_specs = [pl.BlockSpec((tm, tk), lambda i, k: (jnp.where(first(i, k), ai(i, k), n1 - 1),
                                                        jnp.where(first(i, k), ak(i, k), nk - 1))),
                   pl.BlockSpec((tm, tk), lambda i, k: (jnp.where(first(i, k), 0, ai(i, k) - n1),
                                                        jnp.where(first(i, k), 0, ak(i, k))))]
    in_specs = [*a_specs,
                pl.BlockSpec((None, tk, D), lambda i, k: (wl, ak(i, k), 0)),
                xspec, mspec(l, gate_idx), gspec(l)]
    args = [*a_parts, w, x, mod, gpost]
    out_specs = [xspec]
    out_shape = [jax.ShapeDtypeStruct((M, D), _F32)]
    if nxt is not None:
        gnext, gl, ml, sh_idx, sc_idx = nxt
        in_specs += [gspec(gl), mspec(ml, sh_idx), mspec(ml, sc_idx)]
        args += [gnext, mod, mod]
        out_specs.append(xspec)
        out_shape.append(jax.ShapeDtypeStruct((M, D), _BF))
    out = pl.pallas_call(
        functools.partial(_epi_kernel, n_a=len(a_parts), split=n1, nk=nk, te=te,
                          with_next=nxt is not None, rows=cfg.epi_rows),
        grid=(ni, nk + ne),
        in_specs=in_specs,
        out_specs=out_specs,
        out_shape=out_shape,
        scratch_shapes=[pltpu.VMEM((tm, D), _F32)],
        compiler_params=_params(("arbitrary", "arbitrary"), 56),
        name=name,
    )(*args)
    return out if nxt is not None else (out[0], None)


def _rg_kernel(*refs, S, Tc, nchunk, RB, want_final):
    (xb_ref, gt_ref, cw_ref, cb_ref, wa_ref, ba_ref, wx_ref, bx_ref, lam_ref, h0_ref), rest = refs[:10], refs[10:]
    if want_final:
        yg_ref, hfin_ref, xp_s, a_s, u_s, cin_s = rest
    else:
        yg_ref, xp_s, a_s, u_s, cin_s = rest
    R = S * Tc
    T = Tc * nchunk
    C = xb_ref.shape[-1]
    nlt = C // _LANES
    pad = _SUBLANES
    lanes = [slice(n * _LANES, (n + 1) * _LANES) for n in range(nlt)]

    nseq = R // T
    span = T + 2 * pad

    def fill(q, carry):
        base = pl.multiple_of(q * span, pad)
        xp_s[pl.ds(base, pad), :] = jnp.zeros((pad, C), _F32)
        xp_s[pl.ds(base + pad, T), :] = xb_ref[pl.ds(pl.multiple_of(q * T, T), T), :]
        xp_s[pl.ds(base + pad + T, pad), :] = jnp.zeros((pad, C), _F32)
        return carry

    lax.fori_loop(0, nseq, fill, 0)

    wa = (0.5 * wa_ref[...]).astype(_BF)
    wx = (0.5 * wx_ref[...]).astype(_BF)
    ba = 0.5 * ba_ref[...]
    bx = 0.5 * bx_ref[...]
    nceh = (0.5 * _RG_C) * jax.nn.softplus(-lam_ref[...])
    cw = cw_ref[...]
    cb = cb_ref[...]

    def gates(c, carry):
        r0 = pl.multiple_of(c * RB, RB)
        nwin = RB + 2 * pad
        s = r0 // Tc
        t0 = r0 % Tc
        dst = pl.ds(t0 * S + s, RB, stride=S)
        w0 = pl.multiple_of(r0 + (r0 // T) * 2 * pad, pad)
        for n, ln in enumerate(lanes):
            win = xp_s[pl.ds(w0, nwin), ln]
            x0 = win[pad:pad + RB]
            xm1 = pltpu.roll(win, 1, 0)[pad:pad + RB]
            xp1 = pltpu.roll(win, nwin - 1, 0)[pad:pad + RB]
            xp2 = pltpu.roll(win, nwin - 2, 0)[pad:pad + RB]
            xc = cb[:, ln] + xm1 * cw[0:1, ln] + x0 * cw[1:2, ln] + xp1 * cw[2:3, ln] + xp2 * cw[3:4, ln]
            xh = 0.5 * xc
            x16 = xc.astype(_BF)
            for e in range(2):
                tr = jnp.tanh(jnp.dot(x16, wa[e, n], preferred_element_type=_F32) + ba[e][:, ln])
                ti = jnp.tanh(jnp.dot(x16, wx[e, n], preferred_element_type=_F32) + bx[e][:, ln])
                nla = nceh[e][:, ln] * tr + nceh[e][:, ln]
                a = jnp.exp2(nla * (-_LOG2E))
                a_s[e, n, dst, :] = a
                w = jnp.tanh(nla) * (a * a + 1.0)
                sqrt_w = jnp.where(w > 0.0, w * lax.rsqrt(w), 0.0)
                u_s[e, n, dst, :] = sqrt_w * (ti * xh + xh)
        return carry

    lax.fori_loop(0, R // RB, gates, 0)

    chunked = nchunk > 1

    chains = [(e, n) for e in range(2) for n in range(nlt)]
    steps = _SCAN_STEPS

    def scan_block(tb, carry):
        h, p = list(carry[0]), list(carry[1])
        rows = [(pl.ds(pl.multiple_of((tb * steps + k) * S, S), S),
                 pl.ds(pl.multiple_of((Tc - 1 - tb * steps - k) * S, S), S)) for k in range(steps)]
        av = [[a_s[e, n, rows[k][e], :] for k in range(steps)] for e, n in chains]
        uv = [[u_s[e, n, rows[k][e], :] for k in range(steps)] for e, n in chains]
        hs, ps = [], []
        for k in range(steps):
            for i in range(len(chains)):
                h[i] = av[i][k] * h[i] + uv[i][k]
                hs.append(h[i])
                if chunked:
                    p[i] = av[i][k] * p[i]
                    ps.append(p[i])
        for k in range(steps):
            for i, (e, n) in enumerate(chains):
                u_s[e, n, rows[k][e], :] = hs[k * len(chains) + i]
                if chunked:
                    a_s[e, n, rows[k][e], :] = ps[k * len(chains) + i]
        return tuple(h), tuple(p)

    zeros = jnp.zeros((S, _LANES), _F32)
    ones = jnp.ones((S, _LANES), _F32)
    if chunked:
        init = ((zeros,) * (2 * nlt), (ones,) * (2 * nlt))
    else:
        init = (tuple(h0_ref[e][:, ln] for e in range(2) for ln in lanes), ())
    h, p = lax.fori_loop(0, Tc // steps, scan_block, init)

    if want_final:
        for e in range(2):
            for n, ln in enumerate(lanes):
                hfin_ref[e, :, ln] = h[e * nlt + n]

    if chunked:
        chunk_id = lax.broadcasted_iota(jnp.int32, (S, _LANES), 0) % nchunk
        edge = (chunk_id == 0, chunk_id == nchunk - 1)
        shift = (1, S - 1)
        for e in range(2):
            for n, ln in enumerate(lanes):
                h0 = h0_ref[e][:, ln]
                cin = jnp.where(edge[e], h0, 0.0)
                for _ in range(nchunk - 1):
                    cin = jnp.where(edge[e], h0, pltpu.roll(h[e * nlt + n] + p[e * nlt + n] * cin, shift[e], 0))
                cin_s[e, n] = cin

    def emit(c, carry):
        r0 = pl.multiple_of(c * RB, RB)
        s = r0 // Tc
        t0 = r0 % Tc
        src = pl.ds(t0 * S + s, RB, stride=S)
        r = pl.ds(r0, RB)
        for n, ln in enumerate(lanes):
            y = []
            for e in range(2):
                ye = u_s[e, n, src, :]
                if chunked:
                    ye = ye + a_s[e, n, src, :] * cin_s[e, n, pl.ds(s, 1), :]
                y.append(ye)
            yg_ref[r, ln] = ((y[0] + y[1]) * jax.nn.gelu(gt_ref[r, ln])).astype(_BF)
        return carry

    lax.fori_loop(0, R // RB, emit, 0)


def _rg_core(cfg, proj, row_off, S, Tc, nchunk, h0, rgw, j, want_final):
    conv_w, conv_b, w_a, b_a, w_x, b_x, lam = rgw
    rnn = proj.shape[1] // 2
    C = cfg.rg_lanes
    blk = rnn // cfg.rg_blocks
    assert blk == _LANES, "one gate block per lane tile"
    nct = rnn // C
    R = S * Tc
    RB = min(cfg.rg_rows, Tc)
    rb0 = row_off // R
    nrg = conv_w.shape[0]
    vec = lambda arr: arr.reshape(nrg, 2, 1, rnn)
    vspec = pl.BlockSpec((None, 2, 1, C), lambda ct: (j, 0, 0, ct))
    wspec = pl.BlockSpec((None, 2, C // blk, blk, blk), lambda ct: (j, 0, ct, 0, 0))
    out_specs = [pl.BlockSpec((R, C), lambda ct: (0, ct))]
    out_shape = [jax.ShapeDtypeStruct((R, rnn), _BF)]
    if want_final:
        out_specs.append(pl.BlockSpec((2, S, C), lambda ct: (0, 0, ct)))
        out_shape.append(jax.ShapeDtypeStruct((2, S, rnn), _F32))
    return pl.pallas_call(
        functools.partial(_rg_kernel, S=S, Tc=Tc, nchunk=nchunk, RB=RB, want_final=want_final),
        grid=(nct,),
        in_specs=[pl.BlockSpec((R, C), lambda ct: (rb0, nct + ct)),
                  pl.BlockSpec((R, C), lambda ct: (rb0, ct)),
                  pl.BlockSpec((None, _CONV_W, C), lambda ct: (j, 0, ct)),
                  pl.BlockSpec((None, 1, C), lambda ct: (j, 0, ct)),
                  wspec, vspec, wspec, vspec, vspec,
                  pl.BlockSpec((2, S, C), lambda ct: (0, 0, ct))],
        out_specs=out_specs,
        out_shape=out_shape,
        scratch_shapes=[pltpu.VMEM((R + (R // (Tc * nchunk)) * 2 * _SUBLANES, C), _F32),
                        pltpu.VMEM((2, C // _LANES, R, _LANES), _F32),
                        pltpu.VMEM((2, C // _LANES, R, _LANES), _F32),
                        pltpu.VMEM((2, C // _LANES, S, _LANES), _F32)],
        compiler_params=_params(("arbitrary",), 56),
        name="rg_core_ctx" if want_final else "rg_core_lat",
    )(proj, proj, conv_w, conv_b.reshape(nrg, 1, rnn), w_a, vec(b_a), w_x, vec(b_x), vec(lam), h0)


def _attn_kernel(*refs, G, hd, scale, has_cache):
    if has_cache:
        q_ref, k_ref, v_ref, ck_ref, cv_ref, o_ref, k_s, v_s, kc_s, vc_s = refs
    else:
        q_ref, k_ref, v_ref, o_ref, k_s, v_s = refs

    @pl.when(pl.program_id(2) == 0)
    def _cast_keys():
        k_s[...] = k_ref[...].astype(_BF)
        v_s[...] = v_ref[...].astype(_BF)
        if has_cache:
            kc_s[...] = ck_ref[...].astype(_BF)
            vc_s[...] = cv_ref[...].astype(_BF)

    k = k_s[...]
    v = v_s[...]
    if has_cache:
        kc = kc_s[...]
        vc = vc_s[...]
    nt = (((1,), (1,)), ((), ()))
    for g in range(G):
        q = (q_ref[:, g * hd:(g + 1) * hd] * scale).astype(_BF)
        s1 = lax.dot_general(q, k, nt, preferred_element_type=_F32)
        m = jnp.max(s1, axis=-1, keepdims=True)
        if has_cache:
            s2 = lax.dot_general(q, kc, nt, preferred_element_type=_F32)
            m = jnp.maximum(m, jnp.max(s2, axis=-1, keepdims=True))
        p1 = jnp.exp(s1 - m)
        l = jnp.sum(p1, axis=-1, keepdims=True)
        o = jnp.dot(p1.astype(_BF), v, preferred_element_type=_F32)
        if has_cache:
            p2 = jnp.exp(s2 - m)
            l = l + jnp.sum(p2, axis=-1, keepdims=True)
            o = o + jnp.dot(p2.astype(_BF), vc, preferred_element_type=_F32)
        o_ref[:, g * hd:(g + 1) * hd] = (o / l).astype(_BF)


def _attention(cfg, qkv, row_off, B, T, tq, cache, j, name):
    hd = cfg.d_model // cfg.n_heads
    KV = cfg.n_kv_heads
    G = cfg.n_heads // KV
    nq = T // tq
    qb0 = row_off // tq
    kb0 = row_off // T
    kcol = cfg.n_heads
    vcol = cfg.n_heads + KV
    in_specs = [pl.BlockSpec((tq, G * hd), lambda b, h, qi: (qb0 + b * nq + qi, h)),
                pl.BlockSpec((T, hd), lambda b, h, qi: (kb0 + b, kcol + h)),
                pl.BlockSpec((T, hd), lambda b, h, qi: (kb0 + b, vcol + h))]
    args = [qkv, qkv, qkv]
    scratch = [pltpu.VMEM((T, hd), _BF), pltpu.VMEM((T, hd), _BF)]
    if cache is not None:
        ck, cv = cache
        P = ck.shape[2]
        cspec = pl.BlockSpec((None, None, P, hd), lambda b, h, qi: (b, j, 0, h))
        in_specs += [cspec, cspec]
        args += [ck, cv]
        scratch += [pltpu.VMEM((P, hd), _BF), pltpu.VMEM((P, hd), _BF)]
    return pl.pallas_call(
        functools.partial(_attn_kernel, G=G, hd=hd, scale=1.0 / math.sqrt(hd), has_cache=cache is not None),
        grid=(B, KV, nq),
        in_specs=in_specs,
        out_specs=pl.BlockSpec((tq, G * hd), lambda b, h, qi: (b * nq + qi, h)),
        out_shape=jax.ShapeDtypeStruct((B * T, cfg.n_heads * hd), _BF),
        scratch_shapes=scratch,
        compiler_params=_params(("arbitrary", "arbitrary", "arbitrary"), 48),
        name=name,
    )(*args)


def _forward(cfg, x_prompt, x_sample, state_rglru, cache_k, cache_v, c, c_ctx,
             w_mod, b_mod, g_pre_mix, g_post_mix, g_pre_ffn, g_post_ffn,
             w_qkv, g_q, g_k, w_o,
             w_rg_in, rg_conv_w, rg_conv_b, w_rg_a, b_rg_a, w_rg_x, b_rg_x, rg_lambda, w_rg_out,
             w_ff1, w_ff2):
    D = cfg.d_model
    L = cfg.depth
    B, T = cfg.batch, cfg.seq
    DB, DT = cfg.dec_batch, cfg.dec_seq
    NP = B * T
    hd = D // cfg.n_heads
    KV = cfg.n_kv_heads
    nch = cfg.dec_chunks

    cvec = jnp.zeros((_SUBLANES, D), _F32).at[0].set(c_ctx).at[1:1 + DB].set(c)
    mod = _modulation(cfg, cvec, w_mod, b_mod)[:, :1 + DB].reshape(L, 1 + DB, 6, 1, D)

    gains = lambda g: g.reshape(g.shape[0], 1, g.shape[1])
    g_pre_mix, g_post_mix, g_pre_ffn, g_post_ffn = map(gains, (g_pre_mix, g_post_mix, g_pre_ffn, g_post_ffn))
    g_q, g_k = gains(g_q), gains(g_k)
    rope = _rope_tables(cfg)
    ck = cache_k.reshape(DB, -1, cfg.past_len, KV * hd)
    cv = cache_v.reshape(DB, -1, cfg.past_len, KV * hd)

    x, h = _prenorm(cfg, x_prompt.reshape(NP, D), x_sample.reshape(DB * DT, D), g_pre_mix, mod, 0)
    new_states, new_ks, new_vs = [], [], []
    for l in range(L):
        j = l // 2
        if l % 2 == 0:
            proj = _wres_call(_inproj_kernel, cfg, h, w_rg_in, j, cfg.tn_in, _F32, "rg_in_proj")
            rgw = (rg_conv_w, rg_conv_b, w_rg_a, b_rg_a, w_rg_x, b_rg_x, rg_lambda)
            h0p = jnp.zeros((2, B, D), _F32)
            yp, st = _rg_core(cfg, proj, 0, B, T, 1, h0p, rgw, j, True)
            h0s = jnp.repeat(jnp.swapaxes(state_rglru[:, j], 0, 1), nch, axis=1)
            (ys,) = _rg_core(cfg, proj, NP, DB * nch, DT // nch, nch, h0s, rgw, j, False)
            mix = (yp, ys)
            new_states.append(jnp.swapaxes(st, 0, 1))
            w_out, wl = w_rg_out, j
        else:
            qkv = _qkv(cfg, h, w_qkv, g_q, g_k, rope, j)
            kcols = slice(cfg.n_heads * hd, (cfg.n_heads + KV) * hd)
            vcols = slice((cfg.n_heads + KV) * hd, (cfg.n_heads + 2 * KV) * hd)
            new_ks.append(qkv[:NP, kcols].reshape(B, T, KV, hd))
            new_vs.append(qkv[:NP, vcols].reshape(B, T, KV, hd))
            op = _attention(cfg, qkv, 0, B, T, T, None, j, "attn_ctx")
            os_ = _attention(cfg, qkv, NP, DB, DT, cfg.tq, (ck, cv), j, "attn_lat")
            mix = (op, os_)
            w_out, wl = w_o, j
        x, f = _epi(cfg, mix, w_out, wl, x, mod, l, 2, g_post_mix,
                    (g_pre_ffn, l, l, 3, 4), "mix_out_proj")
        hdn = _wres_call(_ffn1_kernel, cfg, f, w_ff1, l, cfg.tn_ff, _BF, "ffn_up")
        nxt = (g_pre_mix, l + 1, l + 1, 0, 1) if l + 1 < L else None
        x, h = _epi(cfg, (hdn,), w_ff2, l, x, mod, l, 5, g_post_ffn, nxt, "ffn_down")

    return (x[:NP].reshape(B, T, D), x[NP:].reshape(DB, DT, D),
            jnp.stack(new_states, axis=1), jnp.stack(new_ks, axis=1), jnp.stack(new_vs, axis=1))


def kernel(x_prompt, x_sample, state_rglru, cache_k, cache_v, c, c_ctx, w_mod, b_mod, g_pre_mix, g_post_mix, g_pre_ffn, g_post_ffn, w_qkv, g_q, g_k, w_o, w_rg_in, rg_conv_w, rg_conv_b, w_rg_a, b_rg_a, w_rg_x, b_rg_x, rg_lambda, w_rg_out, w_ff1, w_ff2):
    return _forward(_CFG, x_prompt, x_sample, state_rglru, cache_k, cache_v, c, c_ctx,
                    w_mod, b_mod, g_pre_mix, g_post_mix, g_pre_ffn, g_post_ffn,
                    w_qkv, g_q, g_k, w_o,
                    w_rg_in, rg_conv_w, rg_conv_b, w_rg_a, b_rg_a, w_rg_x, b_rg_x, rg_lambda, w_rg_out,
                    w_ff1, w_ff2)
```

```python
import functools
import math
from typing import NamedTuple

import jax
import jax.numpy as jnp
from jax import lax
from jax.experimental import pallas as pl
from jax.experimental.pallas import tpu as pltpu

_BF = jnp.bfloat16
_F32 = jnp.float32

_EPS = 1e-6
_RG_C = 8.0
_ROPE_THETA = 10000.0
_LOG2E = 1.4426950408889634
_SCAN_STEPS = 4
_CONV_W = 4
_LANES = 128
_SUBLANES = 8
_MIB = 1024 * 1024


class _Cfg(NamedTuple):
    d_model: int
    batch: int
    seq: int
    depth: int
    dec_batch: int
    dec_seq: int
    past_len: int
    grid_w: int
    n_heads: int
    n_kv_heads: int
    rg_blocks: int
    d_ff: int
    dec_chunks: int
    tm: int
    tm_mm: int
    tm_ff: int
    tn_in: int
    tn_qkv: int
    tn_ff: int
    tm_epi: int
    tk: int
    n_epi: int
    tn_mod: int
    tq: int
    rg_rows: int
    rg_lanes: int
    epi_rows: int


_CFG = _Cfg(d_model=2048, batch=16, seq=256, depth=4, dec_batch=2, dec_seq=2048, past_len=512,
            grid_w=64, n_heads=16, n_kv_heads=4, rg_blocks=16, d_ff=8192, dec_chunks=4,
            tm=512, tm_mm=1024, tm_ff=2048, tn_in=1024, tn_qkv=512, tn_ff=1024, tm_epi=2048, tk=512, n_epi=8, tn_mod=1024, tq=1024,
            rg_rows=256, rg_lanes=256, epi_rows=64)


def _params(sem, vmem_mib):
    return pltpu.CompilerParams(dimension_semantics=sem, vmem_limit_bytes=vmem_mib * _MIB)


def _rms(x, g):
    return x * lax.rsqrt(jnp.mean(x * x, axis=-1, keepdims=True) + _EPS) * g


def _sigmoid(z):
    return 0.5 * jnp.tanh(0.5 * z) + 0.5


def _mod_row(cfg, tm):
    npb = cfg.batch * cfg.seq // tm
    bps = cfg.dec_seq // tm
    return lambda i: jnp.where(i < npb, 0, 1 + (i - npb) // bps)


def _mod_kernel(c_ref, w_ref, b_ref, o_ref):
    s = jax.nn.silu(c_ref[...]).astype(_BF)
    o_ref[...] = jnp.dot(s, w_ref[...].astype(_BF), preferred_element_type=_F32) + b_ref[...]


def _modulation(cfg, cvec, w_mod, b_mod):
    L, D, N = w_mod.shape
    tn = cfg.tn_mod
    return pl.pallas_call(
        _mod_kernel,
        grid=(L, N // tn),
        in_specs=[pl.BlockSpec((_SUBLANES, D), lambda l, j: (0, 0)),
                  pl.BlockSpec((None, D, tn), lambda l, j: (l, 0, j)),
                  pl.BlockSpec((None, 1, tn), lambda l, j: (l, 0, j))],
        out_specs=pl.BlockSpec((None, _SUBLANES, tn), lambda l, j: (l, 0, j)),
        out_shape=jax.ShapeDtypeStruct((L, _SUBLANES, N), _F32),
        compiler_params=_params(("arbitrary", "arbitrary"), 40),
        name="modulation",
    )(cvec, w_mod, b_mod.reshape(L, 1, N))


def _prenorm_kernel(xp_ref, xs_ref, g_ref, sh_ref, sc_ref, x_ref, h_ref, *, n_prompt_blocks):
    def emit(src_ref):
        x = src_ref[...]
        x_ref[...] = x
        h_ref[...] = (_rms(x, g_ref[...]) * (1.0 + sc_ref[...]) + sh_ref[...]).astype(_BF)

    i = pl.program_id(0)
    pl.when(i < n_prompt_blocks)(lambda: emit(xp_ref))
    pl.when(i >= n_prompt_blocks)(lambda: emit(xs_ref))


def _prenorm(cfg, xp, xs, g, mod, l):
    D = xp.shape[1]
    M = xp.shape[0] + xs.shape[0]
    tm = cfg.tm
    npb = xp.shape[0] // tm
    row = _mod_row(cfg, tm)
    mspec = lambda idx: pl.BlockSpec((None, None, None, 1, D), lambda i: (l, row(i), idx, 0, 0))
    out = pl.BlockSpec((tm, D), lambda i: (i, 0))
    return pl.pallas_call(
        functools.partial(_prenorm_kernel, n_prompt_blocks=npb),
        grid=(M // tm,),
        in_specs=[pl.BlockSpec((tm, D), lambda i: (jnp.minimum(i, npb - 1), 0)),
                  pl.BlockSpec((tm, D), lambda i: (jnp.maximum(i - npb, 0), 0)),
                  pl.BlockSpec((None, 1, D), lambda i: (l, 0, 0)),
                  mspec(0), mspec(1)],
        out_specs=[out, out],
        out_shape=[jax.ShapeDtypeStruct((M, D), _F32), jax.ShapeDtypeStruct((M, D), _BF)],
        compiler_params=_params(("arbitrary",), 32),
        name="prenorm",
    )(xp, xs, g, mod, mod)


def _wres_dot(a_ref, w_ref, wbf_ref):
    @pl.when(pl.program_id(1) == 0)
    def _cast():
        wbf_ref[...] = w_ref[...].astype(_BF)
    return jnp.dot(a_ref[...], wbf_ref[...], preferred_element_type=_F32)


def _inproj_kernel(a_ref, w_ref, o_ref, wbf_ref):
    o_ref[...] = _wres_dot(a_ref, w_ref, wbf_ref)


def _ffn1_kernel(a_ref, w_ref, o_ref, wbf_ref):
    z = jnp.maximum(_wres_dot(a_ref, w_ref, wbf_ref), 0.0)
    o_ref[...] = (z * z).astype(_BF)


def _wres_call(kernel, cfg, a, w, l, tn, out_dtype, name, extra_in=(), extra_specs=(), tm=None):
    M, K = a.shape
    N = w.shape[-1]
    tm = cfg.tm_mm if tm is None else tm
    return pl.pallas_call(
        kernel,
        grid=(N // tn, M // tm),
        in_specs=[pl.BlockSpec((tm, K), lambda j, i: (i, 0)),
                  pl.BlockSpec((None, K, tn), lambda j, i: (l, 0, j)),
                  *extra_specs],
        out_specs=pl.BlockSpec((tm, tn), lambda j, i: (i, j)),
        out_shape=jax.ShapeDtypeStruct((M, N), out_dtype),
        scratch_shapes=[pltpu.VMEM((K, tn), _BF)],
        compiler_params=_params(("arbitrary", "arbitrary"), 58),
        name=name,
    )(a, w, *extra_in)


def _rope(z, cos, sin, first_half):
    hd = z.shape[-1]
    quarter = hd // 4
    partner = jnp.where(first_half, pltpu.roll(z, hd - quarter, 1), pltpu.roll(z, quarter, 1))
    return z * cos + partner * sin


def _qkv_kernel(a_ref, w_ref, gq_ref, gk_ref, cos_ref, sin_ref, o_ref, wbf_ref, *,
                n_q_tiles, n_prompt_blocks, hd):
    j = pl.program_id(0)
    i = pl.program_id(1)
    acc = _wres_dot(a_ref, w_ref, wbf_ref)
    heads = acc.shape[-1] // hd
    normed = j <= n_q_tiles
    g = jnp.where(j < n_q_tiles, gq_ref[...], gk_ref[...])

    @pl.when(jnp.logical_not(normed))
    def _v():
        o_ref[...] = acc

    @pl.when(jnp.logical_and(normed, i < n_prompt_blocks))
    def _context():
        for h in range(heads):
            o_ref[:, h * hd:(h + 1) * hd] = _rms(acc[:, h * hd:(h + 1) * hd], g)

    @pl.when(jnp.logical_and(normed, i >= n_prompt_blocks))
    def _latent():
        cos = cos_ref[...]
        sin = sin_ref[...]
        lane = lax.broadcasted_iota(jnp.int32, cos.shape, 1)
        first_half = (lane % (hd // 2)) < (hd // 4)
        for h in range(heads):
            z = _rms(acc[:, h * hd:(h + 1) * hd], g)
            o_ref[:, h * hd:(h + 1) * hd] = _rope(z, cos, sin, first_half)


def _rope_tables(cfg):
    hd = cfg.d_model // cfg.n_heads
    pairs = hd // 4
    n = cfg.dec_seq
    rows = n // cfg.grid_w
    row = jnp.broadcast_to(jnp.arange(rows, dtype=_F32)[:, None], (rows, cfg.grid_w)).reshape(-1)
    col = jnp.broadcast_to(jnp.arange(cfg.grid_w, dtype=_F32)[None, :], (rows, cfg.grid_w)).reshape(-1)
    inv_freq = _ROPE_THETA ** (-jnp.arange(pairs, dtype=_F32) / pairs)
    ang_r = row[:, None] * inv_freq[None, :]
    ang_c = col[:, None] * inv_freq[None, :]
    cos = jnp.concatenate([jnp.cos(ang_r), jnp.cos(ang_r), jnp.cos(ang_c), jnp.cos(ang_c)], axis=-1)
    sin = jnp.concatenate([-jnp.sin(ang_r), jnp.sin(ang_r), -jnp.sin(ang_c), jnp.sin(ang_c)], axis=-1)
    return cos, sin


def _qkv(cfg, h, w_qkv, g_q, g_k, rope, j):
    hd = cfg.d_model // cfg.n_heads
    tm, tn = cfg.tm_mm, cfg.tn_qkv
    npb = cfg.batch * cfg.seq // tm
    bps = cfg.dec_seq // tm
    assert (cfg.n_kv_heads * hd) == tn, "one column tile must hold exactly the k heads"
    tab = pl.BlockSpec((tm, hd), lambda jj, i: (jnp.maximum(i - npb, 0) % bps, 0))
    gain = pl.BlockSpec((None, 1, hd), lambda jj, i: (j, 0, 0))
    kern = functools.partial(_qkv_kernel, n_q_tiles=cfg.n_heads * hd // tn, n_prompt_blocks=npb, hd=hd)
    return _wres_call(kern, cfg, h, w_qkv, j, tn, _F32, "qkv_proj",
                      extra_in=(g_q, g_k, *rope), extra_specs=(gain, gain, tab, tab))


def _epi_kernel(*refs, n_a, split, nk, te, with_next, rows):
    a_refs, refs = refs[:n_a], refs[n_a:]
    if with_next:
        w_ref, x_ref, gate_ref, gpost_ref, gnext_ref, sh_ref, sc_ref, xo_ref, ho_ref, acc_ref = refs
    else:
        w_ref, x_ref, gate_ref, gpost_ref, xo_ref, acc_ref = refs
    i = pl.program_id(0)
    k = pl.program_id(1)

    def accumulate(a_ref, first):
        part = jnp.dot(a_ref[...], w_ref[...].astype(_BF), preferred_element_type=_F32)
        if first:
            acc_ref[...] = part
        else:
            acc_ref[...] += part

    for p, a_ref in enumerate(a_refs):
        mine = True if n_a == 1 else (i < split) == (p == 0)
        pl.when(jnp.logical_and(mine, k == 0))(functools.partial(accumulate, a_ref, True))
        pl.when(jnp.logical_and(mine, jnp.logical_and(k > 0, k < nk)))(functools.partial(accumulate, a_ref, False))

    @pl.when(k >= nk)
    def _epilogue():
        base = (k - nk) * te
        gain = gate_ref[...] * gpost_ref[...]
        if with_next:
            gain_next = gnext_ref[...] * (1.0 + sc_ref[...])
        for c in range(te // rows):
            r = slice(c * rows, (c + 1) * rows)
            ra = pl.ds(pl.multiple_of(base + c * rows, rows), rows)
            xn = x_ref[r, :] + _rms(acc_ref[ra, :], gain)
            xo_ref[r, :] = xn
            if with_next:
                ho_ref[r, :] = (_rms(xn, gain_next) + sh_ref[...]).astype(_BF)


def _epi(cfg, a_parts, w, wl, x, mod, l, gate_idx, gpost, nxt, name):
    K = a_parts[0].shape[1]
    M = sum(p.shape[0] for p in a_parts)
    D = w.shape[-1]
    tm, tk, ne = cfg.tm_epi, cfg.tk, cfg.n_epi
    te = tm // ne
    nk = K // tk
    ni = M // tm
    row = _mod_row(cfg, tm)
    ai = lambda i, k: jnp.where(k < nk, i, jnp.minimum(i + 1, ni - 1))
    ak = lambda i, k: jnp.where(k < nk, k, jnp.where(i + 1 < ni, 0, nk - 1))
    mspec = lambda ml, idx: pl.BlockSpec((None, None, None, 1, D), lambda i, k: (ml, row(i), idx, 0, 0))
    gspec = lambda gl: pl.BlockSpec((None, 1, D), lambda i, k: (gl, 0, 0))
    xspec = pl.BlockSpec((te, D), lambda i, k: (i * ne + jnp.clip(k - nk, 0, ne - 1), 0))
    n1 = a_parts[0].shape[0] // tm
    if len(a_parts) == 1:
        a_specs = [pl.BlockSpec((tm, tk), lambda i, k: (ai(i, k), ak(i, k)))]
    else:
        first = lambda i, k: ai(i, k) < n1
        a_specs = [pl.BlockSpec((tm, tk), lambda i, k: (jnp.where(first(i, k), ai(i, k), n1 - 1),
                                                        jnp.where(first(i, k), ak(i, k), nk - 1))),
                   pl.BlockSpec((tm, tk), lambda i, k: (jnp.where(first(i, k), 0, ai(i, k) - n1),
                                                        jnp.where(first(i, k), 0, ak(i, k))))]
    in_specs = [*a_specs,
                pl.BlockSpec((None, tk, D), lambda i, k: (wl, ak(i, k), 0)),
                xspec, mspec(l, gate_idx), gspec(l)]
    args = [*a_parts, w, x, mod, gpost]
    out_specs = [xspec]
    out_shape = [jax.ShapeDtypeStruct((M, D), _F32)]
    if nxt is not None:
        gnext, gl, ml, sh_idx, sc_idx = nxt
        in_specs += [gspec(gl), mspec(ml, sh_idx), mspec(ml, sc_idx)]
        args += [gnext, mod, mod]
        out_specs.append(xspec)
        out_shape.append(jax.ShapeDtypeStruct((M, D), _BF))
    out = pl.pallas_call(
        functools.partial(_epi_kernel, n_a=len(a_parts), split=n1, nk=nk, te=te,
                          with_next=nxt is not None, rows=cfg.epi_rows),
        grid=(ni, nk + ne),
        in_specs=in_specs,
        out_specs=out_specs,
        out_shape=out_shape,
        scratch_shapes=[pltpu.VMEM((tm, D), _F32)],
        compiler_params=_params(("arbitrary", "arbitrary"), 56),
        name=name,
    )(*args)
    return out if nxt is not None else (out[0], None)


def _rg_kernel(*refs, S, Tc, nchunk, RB, want_final):
    (xb_ref, gt_ref, cw_ref, cb_ref, wa_ref, ba_ref, wx_ref, bx_ref, lam_ref, h0_ref), rest = refs[:10], refs[10:]
    if want_final:
        yg_ref, hfin_ref, xp_s, a_s, u_s, cin_s = rest
    else:
        yg_ref, xp_s, a_s, u_s, cin_s = rest
    R = S * Tc
    T = Tc * nchunk
    C = xb_ref.shape[-1]
    nlt = C // _LANES
    pad = _SUBLANES
    lanes = [slice(n * _LANES, (n + 1) * _LANES) for n in range(nlt)]

    nseq = R // T
    span = T + 2 * pad

    def fill(q, carry):
        base = pl.multiple_of(q * span, pad)
        xp_s[pl.ds(base, pad), :] = jnp.zeros((pad, C), _F32)
        xp_s[pl.ds(base + pad, T), :] = xb_ref[pl.ds(pl.multiple_of(q * T, T), T), :]
        xp_s[pl.ds(base + pad + T, pad), :] = jnp.zeros((pad, C), _F32)
        return carry

    lax.fori_loop(0, nseq, fill, 0)

    wa = (0.5 * wa_ref[...]).astype(_BF)
    wx = (0.5 * wx_ref[...]).astype(_BF)
    ba = 0.5 * ba_ref[...]
    bx = 0.5 * bx_ref[...]
    nceh = (0.5 * _RG_C) * jax.nn.softplus(-lam_ref[...])
    cw = cw_ref[...]
    cb = cb_ref[...]

    def gates(c, carry):
        r0 = pl.multiple_of(c * RB, RB)
        nwin = RB + 2 * pad
        s = r0 // Tc
        t0 = r0 % Tc
        dst = pl.ds(t0 * S + s, RB, stride=S)
        w0 = pl.multiple_of(r0 + (r0 // T) * 2 * pad, pad)
        for n, ln in enumerate(lanes):
            win = xp_s[pl.ds(w0, nwin), ln]
            x0 = win[pad:pad + RB]
            xm1 = pltpu.roll(win, 1, 0)[pad:pad + RB]
            xp1 = pltpu.roll(win, nwin - 1, 0)[pad:pad + RB]
            xp2 = pltpu.roll(win, nwin - 2, 0)[pad:pad + RB]
            xc = cb[:, ln] + xm1 * cw[0:1, ln] + x0 * cw[1:2, ln] + xp1 * cw[2:3, ln] + xp2 * cw[3:4, ln]
            xh = 0.5 * xc
            x16 = xc.astype(_BF)
            for e in range(2):
                tr = jnp.tanh(jnp.dot(x16, wa[e, n], preferred_element_type=_F32) + ba[e][:, ln])
                ti = jnp.tanh(jnp.dot(x16, wx[e, n], preferred_element_type=_F32) + bx[e][:, ln])
                nla = nceh[e][:, ln] * tr + nceh[e][:, ln]
                a = jnp.exp2(nla * (-_LOG2E))
                a_s[e, n, dst, :] = a
                w = jnp.tanh(nla) * (a * a + 1.0)
                sqrt_w = jnp.where(w > 0.0, w * lax.rsqrt(w), 0.0)
                u_s[e, n, dst, :] = sqrt_w * (ti * xh + xh)
        return carry

    lax.fori_loop(0, R // RB, gates, 0)

    chunked = nchunk > 1

    chains = [(e, n) for e in range(2) for n in range(nlt)]
    steps = _SCAN_STEPS

    def scan_block(tb, carry):
        h, p = list(carry[0]), list(carry[1])
        rows = [(pl.ds(pl.multiple_of((tb * steps + k) * S, S), S),
                 pl.ds(pl.multiple_of((Tc - 1 - tb * steps - k) * S, S), S)) for k in range(steps)]
        av = [[a_s[e, n, rows[k][e], :] for k in range(steps)] for e, n in chains]
        uv = [[u_s[e, n, rows[k][e], :] for k in range(steps)] for e, n in chains]
        hs, ps = [], []
        for k in range(steps):
            for i in range(len(chains)):
                h[i] = av[i][k] * h[i] + uv[i][k]
                hs.append(h[i])
                if chunked:
                    p[i] = av[i][k] * p[i]
                    ps.append(p[i])
        for k in range(steps):
            for i, (e, n) in enumerate(chains):
                u_s[e, n, rows[k][e], :] = hs[k * len(chains) + i]
                if chunked:
                    a_s[e, n, rows[k][e], :] = ps[k * len(chains) + i]
        return tuple(h), tuple(p)

    zeros = jnp.zeros((S, _LANES), _F32)
    ones = jnp.ones((S, _LANES), _F32)
    if chunked:
        init = ((zeros,) * (2 * nlt), (ones,) * (2 * nlt))
    else:
        init = (tuple(h0_ref[e][:, ln] for e in range(2) for ln in lanes), ())
    h, p = lax.fori_loop(0, Tc // steps, scan_block, init)

    if want_final:
        for e in range(2):
            for n, ln in enumerate(lanes):
                hfin_ref[e, :, ln] = h[e * nlt + n]

    if chunked:
        chunk_id = lax.broadcasted_iota(jnp.int32, (S, _LANES), 0) % nchunk
        edge = (chunk_id == 0, chunk_id == nchunk - 1)
        shift = (1, S - 1)
        for e in range(2):
            for n, ln in enumerate(lanes):
                h0 = h0_ref[e][:, ln]
                cin = jnp.where(edge[e], h0, 0.0)
                for _ in range(nchunk - 1):
                    cin = jnp.where(edge[e], h0, pltpu.roll(h[e * nlt + n] + p[e * nlt + n] * cin, shift[e], 0))
                cin_s[e, n] = cin

    def emit(c, carry):
        r0 = pl.multiple_of(c * RB, RB)
        s = r0 // Tc
        t0 = r0 % Tc
        src = pl.ds(t0 * S + s, RB, stride=S)
        r = pl.ds(r0, RB)
        for n, ln in enumerate(lanes):
            y = []
            for e in range(2):
                ye = u_s[e, n, src, :]
                if chunked:
                    ye = ye + a_s[e, n, src, :] * cin_s[e, n, pl.ds(s, 1), :]
                y.append(ye)
            yg_ref[r, ln] = ((y[0] + y[1]) * jax.nn.gelu(gt_ref[r, ln])).astype(_BF)
        return carry

    lax.fori_loop(0, R // RB, emit, 0)


def _rg_core(cfg, proj, row_off, S, Tc, nchunk, h0, rgw, j, want_final):
    conv_w, conv_b, w_a, b_a, w_x, b_x, lam = rgw
    rnn = proj.shape[1] // 2
    C = cfg.rg_lanes
    blk = rnn // cfg.rg_blocks
    assert blk == _LANES, "one gate block per lane tile"
    nct = rnn // C
    R = S * Tc
    RB = min(cfg.rg_rows, Tc)
    rb0 = row_off // R
    nrg = conv_w.shape[0]
    vec = lambda arr: arr.reshape(nrg, 2, 1, rnn)
    vspec = pl.BlockSpec((None, 2, 1, C), lambda ct: (j, 0, 0, ct))
    wspec = pl.BlockSpec((None, 2, C // blk, blk, blk), lambda ct: (j, 0, ct, 0, 0))
    out_specs = [pl.BlockSpec((R, C), lambda ct: (0, ct))]
    out_shape = [jax.ShapeDtypeStruct((R, rnn), _BF)]
    if want_final:
        out_specs.append(pl.BlockSpec((2, S, C), lambda ct: (0, 0, ct)))
        out_shape.append(jax.ShapeDtypeStruct((2, S, rnn), _F32))
    return pl.pallas_call(
        functools.partial(_rg_kernel, S=S, Tc=Tc, nchunk=nchunk, RB=RB, want_final=want_final),
        grid=(nct,),
        in_specs=[pl.BlockSpec((R, C), lambda ct: (rb0, nct + ct)),
                  pl.BlockSpec((R, C), lambda ct: (rb0, ct)),
                  pl.BlockSpec((None, _CONV_W, C), lambda ct: (j, 0, ct)),
                  pl.BlockSpec((None, 1, C), lambda ct: (j, 0, ct)),
                  wspec, vspec, wspec, vspec, vspec,
                  pl.BlockSpec((2, S, C), lambda ct: (0, 0, ct))],
        out_specs=out_specs,
        out_shape=out_shape,
        scratch_shapes=[pltpu.VMEM((R + (R // (Tc * nchunk)) * 2 * _SUBLANES, C), _F32),
                        pltpu.VMEM((2, C // _LANES, R, _LANES), _F32),
                        pltpu.VMEM((2, C // _LANES, R, _LANES), _F32),
                        pltpu.VMEM((2, C // _LANES, S, _LANES), _F32)],
        compiler_params=_params(("arbitrary",), 56),
        name="rg_core_ctx" if want_final else "rg_core_lat",
    )(proj, proj, conv_w, conv_b.reshape(nrg, 1, rnn), w_a, vec(b_a), w_x, vec(b_x), vec(lam), h0)


def _attn_kernel(*refs, G, hd, scale, has_cache):
    if has_cache:
        q_ref, k_ref, v_ref, ck_ref, cv_ref, o_ref, k_s, v_s, kc_s, vc_s = refs
    else:
        q_ref, k_ref, v_ref, o_ref, k_s, v_s = refs

    @pl.when(pl.program_id(2) == 0)
    def _cast_keys():
        k_s[...] = k_ref[...].astype(_BF)
        v_s[...] = v_ref[...].astype(_BF)
        if has_cache:
            kc_s[...] = ck_ref[...].astype(_BF)
            vc_s[...] = cv_ref[...].astype(_BF)

    k = k_s[...]
    v = v_s[...]
    if has_cache:
        kc = kc_s[...]
        vc = vc_s[...]
    nt = (((1,), (1,)), ((), ()))
    for g in range(G):
        q = (q_ref[:, g * hd:(g + 1) * hd] * scale).astype(_BF)
        s1 = lax.dot_general(q, k, nt, preferred_element_type=_F32)
        m = jnp.max(s1, axis=-1, keepdims=True)
        if has_cache:
            s2 = lax.dot_general(q, kc, nt, preferred_element_type=_F32)
            m = jnp.maximum(m, jnp.max(s2, axis=-1, keepdims=True))
        p1 = jnp.exp(s1 - m)
        l = jnp.sum(p1, axis=-1, keepdims=True)
        o = jnp.dot(p1.astype(_BF), v, preferred_element_type=_F32)
        if has_cache:
            p2 = jnp.exp(s2 - m)
            l = l + jnp.sum(p2, axis=-1, keepdims=True)
            o = o + jnp.dot(p2.astype(_BF), vc, preferred_element_type=_F32)
        o_ref[:, g * hd:(g + 1) * hd] = (o / l).astype(_BF)


def _attention(cfg, qkv, row_off, B, T, tq, cache, j, name):
    hd = cfg.d_model // cfg.n_heads
    KV = cfg.n_kv_heads
    G = cfg.n_heads // KV
    nq = T // tq
    qb0 = row_off // tq
    kb0 = row_off // T
    kcol = cfg.n_heads
    vcol = cfg.n_heads + KV
    in_specs = [pl.BlockSpec((tq, G * hd), lambda b, h, qi: (qb0 + b * nq + qi, h)),
                pl.BlockSpec((T, hd), lambda b, h, qi: (kb0 + b, kcol + h)),
                pl.BlockSpec((T, hd), lambda b, h, qi: (kb0 + b, vcol + h))]
    args = [qkv, qkv, qkv]
    scratch = [pltpu.VMEM((T, hd), _BF), pltpu.VMEM((T, hd), _BF)]
    if cache is not None:
        ck, cv = cache
        P = ck.shape[2]
        cspec = pl.BlockSpec((None, None, P, hd), lambda b, h, qi: (b, j, 0, h))
        in_specs += [cspec, cspec]
        args += [ck, cv]
        scratch += [pltpu.VMEM((P, hd), _BF), pltpu.VMEM((P, hd), _BF)]
    return pl.pallas_call(
        functools.partial(_attn_kernel, G=G, hd=hd, scale=1.0 / math.sqrt(hd), has_cache=cache is not None),
        grid=(B, KV, nq),
        in_specs=in_specs,
        out_specs=pl.BlockSpec((tq, G * hd), lambda b, h, qi: (b * nq + qi, h)),
        out_shape=jax.ShapeDtypeStruct((B * T, cfg.n_heads * hd), _BF),
        scratch_shapes=scratch,
        compiler_params=_params(("arbitrary", "arbitrary", "arbitrary"), 58),
        name=name,
    )(*args)


def _forward(cfg, x_prompt, x_sample, state_rglru, cache_k, cache_v, c, c_ctx,
             w_mod, b_mod, g_pre_mix, g_post_mix, g_pre_ffn, g_post_ffn,
             w_qkv, g_q, g_k, w_o,
             w_rg_in, rg_conv_w, rg_conv_b, w_rg_a, b_rg_a, w_rg_x, b_rg_x, rg_lambda, w_rg_out,
             w_ff1, w_ff2):
    D = cfg.d_model
    L = cfg.depth
    B, T = cfg.batch, cfg.seq
    DB, DT = cfg.dec_batch, cfg.dec_seq
    NP = B * T
    hd = D // cfg.n_heads
    KV = cfg.n_kv_heads
    nch = cfg.dec_chunks

    cvec = jnp.zeros((_SUBLANES, D), _F32).at[0].set(c_ctx).at[1:1 + DB].set(c)
    mod = _modulation(cfg, cvec, w_mod, b_mod)[:, :1 + DB].reshape(L, 1 + DB, 6, 1, D)

    gains = lambda g: g.reshape(g.shape[0], 1, g.shape[1])
    g_pre_mix, g_post_mix, g_pre_ffn, g_post_ffn = map(gains, (g_pre_mix, g_post_mix, g_pre_ffn, g_post_ffn))
    g_q, g_k = gains(g_q), gains(g_k)
    rope = _rope_tables(cfg)
    ck = cache_k.reshape(DB, -1, cfg.past_len, KV * hd)
    cv = cache_v.reshape(DB, -1, cfg.past_len, KV * hd)

    x, h = _prenorm(cfg, x_prompt.reshape(NP, D), x_sample.reshape(DB * DT, D), g_pre_mix, mod, 0)
    new_states, new_ks, new_vs = [], [], []
    for l in range(L):
        j = l // 2
        if l % 2 == 0:
            proj = _wres_call(_inproj_kernel, cfg, h, w_rg_in, j, cfg.tn_in, _F32, "rg_in_proj")
            rgw = (rg_conv_w, rg_conv_b, w_rg_a, b_rg_a, w_rg_x, b_rg_x, rg_lambda)
            h0p = jnp.zeros((2, B, D), _F32)
            yp, st = _rg_core(cfg, proj, 0, B, T, 1, h0p, rgw, j, True)
            h0s = jnp.repeat(jnp.swapaxes(state_rglru[:, j], 0, 1), nch, axis=1)
            (ys,) = _rg_core(cfg, proj, NP, DB * nch, DT // nch, nch, h0s, rgw, j, False)
            mix = (yp, ys)
            new_states.append(jnp.swapaxes(st, 0, 1))
            w_out, wl = w_rg_out, j
        else:
            qkv = _qkv(cfg, h, w_qkv, g_q, g_k, rope, j)
            kcols = slice(cfg.n_heads * hd, (cfg.n_heads + KV) * hd)
            vcols = slice((cfg.n_heads + KV) * hd, (cfg.n_heads + 2 * KV) * hd)
            new_ks.append(qkv[:NP, kcols].reshape(B, T, KV, hd))
            new_vs.append(qkv[:NP, vcols].reshape(B, T, KV, hd))
            op = _attention(cfg, qkv, 0, B, T, T, None, j, "attn_ctx")
            os_ = _attention(cfg, qkv, NP, DB, DT, cfg.tq, (ck, cv), j, "attn_lat")
            mix = (op, os_)
            w_out, wl = w_o, j
        x, f = _epi(cfg, mix, w_out, wl, x, mod, l, 2, g_post_mix,
                    (g_pre_ffn, l, l, 3, 4), "mix_out_proj")
        hdn = _wres_call(_ffn1_kernel, cfg, f, w_ff1, l, cfg.tn_ff, _BF, "ffn_up", tm=cfg.tm_ff)
        nxt = (g_pre_mix, l + 1, l + 1, 0, 1) if l + 1 < L else None
        x, h = _epi(cfg, (hdn,), w_ff2, l, x, mod, l, 5, g_post_ffn, nxt, "ffn_down")

    return (x[:NP].reshape(B, T, D), x[NP:].reshape(DB, DT, D),
            jnp.stack(new_states, axis=1), jnp.stack(new_ks, axis=1), jnp.stack(new_vs, axis=1))


def kernel(x_prompt, x_sample, state_rglru, cache_k, cache_v, c, c_ctx, w_mod, b_mod, g_pre_mix, g_post_mix, g_pre_ffn, g_post_ffn, w_qkv, g_q, g_k, w_o, w_rg_in, rg_conv_w, rg_conv_b, w_rg_a, b_rg_a, w_rg_x, b_rg_x, rg_lambda, w_rg_out, w_ff1, w_ff2):
    return _forward(_CFG, x_prompt, x_sample, state_rglru, cache_k, cache_v, c, c_ctx,
                    w_mod, b_mod, g_pre_mix, g_post_mix, g_pre_ffn, g_post_ffn,
                    w_qkv, g_q, g_k, w_o,
                    w_rg_in, rg_conv_w, rg_conv_b, w_rg_a, b_rg_a, w_rg_x, b_rg_x, rg_lambda, w_rg_out,
                    w_ff1, w_ff2)
```

```python
import functools
import math
from typing import NamedTuple

import jax
import jax.numpy as jnp
from jax import lax
from jax.experimental import pallas as pl
from jax.experimental.pallas import tpu as pltpu

_BF = jnp.bfloat16
_F32 = jnp.float32

_EPS = 1e-6
_RG_C = 8.0
_ROPE_THETA = 10000.0
_LOG2E = 1.4426950408889634
_SCAN_STEPS = 4
_CONV_W = 4
_LANES = 128
_SUBLANES = 8
_MIB = 1024 * 1024


class _Cfg(NamedTuple):
    d_model: int
    batch: int
    seq: int
    depth: int
    dec_batch: int
    dec_seq: int
    past_len: int
    grid_w: int
    n_heads: int
    n_kv_heads: int
    rg_blocks: int
    d_ff: int
    dec_chunks: int
    tm: int
    tm_mm: int
    tm_ff: int
    tn_in: int
    tn_qkv: int
    tn_ff: int
    tm_epi: int
    tk: int
    n_epi: int
    tn_mod: int
    tq: int
    rg_rows: int
    rg_lanes: int
    epi_rows: int


_CFG = _Cfg(d_model=2048, batch=16, seq=256, depth=4, dec_batch=2, dec_seq=2048, past_len=512,
            grid_w=64, n_heads=16, n_kv_heads=4, rg_blocks=16, d_ff=8192, dec_chunks=4,
            tm=512, tm_mm=1024, tm_ff=2048, tn_in=1024, tn_qkv=512, tn_ff=1024, tm_epi=2048, tk=512, n_epi=8, tn_mod=1024, tq=1024,
            rg_rows=256, rg_lanes=256, epi_rows=64)


def _params(sem, vmem_mib):
    return pltpu.CompilerParams(dimension_semantics=sem, vmem_limit_bytes=vmem_mib * _MIB)


def _rms(x, g):
    return x * lax.rsqrt(jnp.mean(x * x, axis=-1, keepdims=True) + _EPS) * g


def _sigmoid(z):
    return 0.5 * jnp.tanh(0.5 * z) + 0.5


def _mod_row(cfg, tm):
    npb = cfg.batch * cfg.seq // tm
    bps = cfg.dec_seq // tm
    return lambda i: jnp.where(i < npb, 0, 1 + (i - npb) // bps)


def _mod_kernel(c_ref, w_ref, b_ref, o_ref):
    s = jax.nn.silu(c_ref[...]).astype(_BF)
    o_ref[...] = jnp.dot(s, w_ref[...].astype(_BF), preferred_element_type=_F32) + b_ref[...]


def _modulation(cfg, cvec, w_mod, b_mod):
    L, D, N = w_mod.shape
    tn = cfg.tn_mod
    return pl.pallas_call(
        _mod_kernel,
        grid=(L, N // tn),
        in_specs=[pl.BlockSpec((_SUBLANES, D), lambda l, j: (0, 0)),
                  pl.BlockSpec((None, D, tn), lambda l, j: (l, 0, j)),
                  pl.BlockSpec((None, 1, tn), lambda l, j: (l, 0, j))],
        out_specs=pl.BlockSpec((None, _SUBLANES, tn), lambda l, j: (l, 0, j)),
        out_shape=jax.ShapeDtypeStruct((L, _SUBLANES, N), _F32),
        compiler_params=_params(("arbitrary", "arbitrary"), 40),
        name="modulation",
    )(cvec, w_mod, b_mod.reshape(L, 1, N))


def _prenorm_kernel(xp_ref, xs_ref, g_ref, sh_ref, sc_ref, x_ref, h_ref, *, n_prompt_blocks):
    def emit(src_ref):
        x = src_ref[...]
        x_ref[...] = x
        h_ref[...] = (_rms(x, g_ref[...]) * (1.0 + sc_ref[...]) + sh_ref[...]).astype(_BF)

    i = pl.program_id(0)
    pl.when(i < n_prompt_blocks)(lambda: emit(xp_ref))
    pl.when(i >= n_prompt_blocks)(lambda: emit(xs_ref))


def _prenorm(cfg, xp, xs, g, mod, l):
    D = xp.shape[1]
    M = xp.shape[0] + xs.shape[0]
    tm = cfg.tm
    npb = xp.shape[0] // tm
    row = _mod_row(cfg, tm)
    mspec = lambda idx: pl.BlockSpec((None, None, None, 1, D), lambda i: (l, row(i), idx, 0, 0))
    out = pl.BlockSpec((tm, D), lambda i: (i, 0))
    return pl.pallas_call(
        functools.partial(_prenorm_kernel, n_prompt_blocks=npb),
        grid=(M // tm,),
        in_specs=[pl.BlockSpec((tm, D), lambda i: (jnp.minimum(i, npb - 1), 0)),
                  pl.BlockSpec((tm, D), lambda i: (jnp.maximum(i - npb, 0), 0)),
                  pl.BlockSpec((None, 1, D), lambda i: (l, 0, 0)),
                  mspec(0), mspec(1)],
        out_specs=[out, out],
        out_shape=[jax.ShapeDtypeStruct((M, D), _F32), jax.ShapeDtypeStruct((M, D), _BF)],
        compiler_params=_params(("arbitrary",), 32),
        name="prenorm",
    )(xp, xs, g, mod, mod)


def _wres_dot(a_ref, w_ref, wbf_ref):
    @pl.when(pl.program_id(1) == 0)
    def _cast():
        wbf_ref[...] = w_ref[...].astype(_BF)
    return jnp.dot(a_ref[...], wbf_ref[...], preferred_element_type=_F32)


def _inproj_kernel(a_ref, w_ref, o_ref, wbf_ref):
    o_ref[...] = _wres_dot(a_ref, w_ref, wbf_ref)


def _ffn1_kernel(a_ref, w_ref, o_ref, wbf_ref):
    z = jnp.maximum(_wres_dot(a_ref, w_ref, wbf_ref), 0.0)
    o_ref[...] = (z * z).astype(_BF)


def _wres_call(kernel, cfg, a, w, l, tn, out_dtype, name, extra_in=(), extra_specs=(), tm=None):
    M, K = a.shape
    N = w.shape[-1]
    tm = cfg.tm_mm if tm is None else tm
    return pl.pallas_call(
        kernel,
        grid=(N // tn, M // tm),
        in_specs=[pl.BlockSpec((tm, K), lambda j, i: (i, 0)),
                  pl.BlockSpec((None, K, tn), lambda j, i: (l, 0, j)),
                  *extra_specs],
        out_specs=pl.BlockSpec((tm, tn), lambda j, i: (i, j)),
        out_shape=jax.ShapeDtypeStruct((M, N), out_dtype),
        scratch_shapes=[pltpu.VMEM((K, tn), _BF)],
        compiler_params=_params(("arbitrary", "arbitrary"), 58),
        name=name,
    )(a, w, *extra_in)


def _rope(z, cos, sin, first_half):
    hd = z.shape[-1]
    quarter = hd // 4
    partner = jnp.where(first_half, pltpu.roll(z, hd - quarter, 1), pltpu.roll(z, quarter, 1))
    return z * cos + partner * sin


def _qkv_kernel(a_ref, w_ref, gq_ref, gk_ref, cos_ref, sin_ref, o_ref, wbf_ref, *,
                n_q_tiles, n_prompt_blocks, hd):
    j = pl.program_id(0)
    i = pl.program_id(1)
    acc = _wres_dot(a_ref, w_ref, wbf_ref)
    heads = acc.shape[-1] // hd
    normed = j <= n_q_tiles
    g = jnp.where(j < n_q_tiles, gq_ref[...], gk_ref[...])

    @pl.when(jnp.logical_not(normed))
    def _v():
        o_ref[...] = acc

    @pl.when(jnp.logical_and(normed, i < n_prompt_blocks))
    def _context():
        for h in range(heads):
            o_ref[:, h * hd:(h + 1) * hd] = _rms(acc[:, h * hd:(h + 1) * hd], g)

    @pl.when(jnp.logical_and(normed, i >= n_prompt_blocks))
    def _latent():
        cos = cos_ref[...]
        sin = sin_ref[...]
        lane = lax.broadcasted_iota(jnp.int32, cos.shape, 1)
        first_half = (lane % (hd // 2)) < (hd // 4)
        for h in range(heads):
            z = _rms(acc[:, h * hd:(h + 1) * hd], g)
            o_ref[:, h * hd:(h + 1) * hd] = _rope(z, cos, sin, first_half)


def _rope_tables(cfg):
    hd = cfg.d_model // cfg.n_heads
    pairs = hd // 4
    n = cfg.dec_seq
    rows = n // cfg.grid_w
    row = jnp.broadcast_to(jnp.arange(rows, dtype=_F32)[:, None], (rows, cfg.grid_w)).reshape(-1)
    col = jnp.broadcast_to(jnp.arange(cfg.grid_w, dtype=_F32)[None, :], (rows, cfg.grid_w)).reshape(-1)
    inv_freq = _ROPE_THETA ** (-jnp.arange(pairs, dtype=_F32) / pairs)
    ang_r = row[:, None] * inv_freq[None, :]
    ang_c = col[:, None] * inv_freq[None, :]
    cos = jnp.concatenate([jnp.cos(ang_r), jnp.cos(ang_r), jnp.cos(ang_c), jnp.cos(ang_c)], axis=-1)
    sin = jnp.concatenate([-jnp.sin(ang_r), jnp.sin(ang_r), -jnp.sin(ang_c), jnp.sin(ang_c)], axis=-1)
    return cos, sin


def _qkv(cfg, h, w_qkv, g_q, g_k, rope, j):
    hd = cfg.d_model // cfg.n_heads
    tm, tn = cfg.tm_mm, cfg.tn_qkv
    npb = cfg.batch * cfg.seq // tm
    bps = cfg.dec_seq // tm
    assert (cfg.n_kv_heads * hd) == tn, "one column tile must hold exactly the k heads"
    tab = pl.BlockSpec((tm, hd), lambda jj, i: (jnp.maximum(i - npb, 0) % bps, 0))
    gain = pl.BlockSpec((None, 1, hd), lambda jj, i: (j, 0, 0))
    kern = functools.partial(_qkv_kernel, n_q_tiles=cfg.n_heads * hd // tn, n_prompt_blocks=npb, hd=hd)
    return _wres_call(kern, cfg, h, w_qkv, j, tn, _F32, "qkv_proj",
                      extra_in=(g_q, g_k, *rope), extra_specs=(gain, gain, tab, tab))


def _epi_kernel(*refs, n_a, split, nk, te, with_next, rows):
    a_refs, refs = refs[:n_a], refs[n_a:]
    if with_next:
        w_ref, x_ref, gate_ref, gpost_ref, gnext_ref, sh_ref, sc_ref, xo_ref, ho_ref, acc_ref = refs
    else:
        w_ref, x_ref, gate_ref, gpost_ref, xo_ref, acc_ref = refs
    i = pl.program_id(0)
    k = pl.program_id(1)

    def accumulate(a_ref, first):
        part = jnp.dot(a_ref[...], w_ref[...].astype(_BF), preferred_element_type=_F32)
        if first:
            acc_ref[...] = part
        else:
            acc_ref[...] += part

    for p, a_ref in enumerate(a_refs):
        mine = True if n_a == 1 else (i < split) == (p == 0)
        pl.when(jnp.logical_and(mine, k == 0))(functools.partial(accumulate, a_ref, True))
        pl.when(jnp.logical_and(mine, jnp.logical_and(k > 0, k < nk)))(functools.partial(accumulate, a_ref, False))

    @pl.when(k >= nk)
    def _epilogue():
        base = (k - nk) * te
        gain = gate_ref[...] * gpost_ref[...]
        if with_next:
            gain_next = gnext_ref[...] * (1.0 + sc_ref[...])
        for c in range(te // rows):
            r = slice(c * rows, (c + 1) * rows)
            ra = pl.ds(pl.multiple_of(base + c * rows, rows), rows)
            xn = x_ref[r, :] + _rms(acc_ref[ra, :], gain)
            xo_ref[r, :] = xn
            if with_next:
                ho_ref[r, :] = (_rms(xn, gain_next) + sh_ref[...]).astype(_BF)


def _epi(cfg, a_parts, w, wl, x, mod, l, gate_idx, gpost, nxt, name):
    K = a_parts[0].shape[1]
    M = sum(p.shape[0] for p in a_parts)
    D = w.shape[-1]
    tm, tk, ne = cfg.tm_epi, cfg.tk, cfg.n_epi
    te = tm // ne
    nk = K // tk
    ni = M // tm
    row = _mod_row(cfg, tm)
    ai = lambda i, k: jnp.where(k < nk, i, jnp.minimum(i + 1, ni - 1))
    ak = lambda i, k: jnp.where(k < nk, k, jnp.where(i + 1 < ni, 0, nk - 1))
    mspec = lambda ml, idx: pl.BlockSpec((None, None, None, 1, D), lambda i, k: (ml, row(i), idx, 0, 0))
    gspec = lambda gl: pl.BlockSpec((None, 1, D), lambda i, k: (gl, 0, 0))
    xspec = pl.BlockSpec((te, D), lambda i, k: (i * ne + jnp.clip(k - nk, 0, ne - 1), 0))
    n1 = a_parts[0].shape[0] // tm
    if len(a_parts) == 1:
        a_specs = [pl.BlockSpec((tm, tk), lambda i, k: (ai(i, k), ak(i, k)))]
    else:
        first = lambda i, k: ai(i, k) < n1
        a_specs = [pl.BlockSpec((tm, tk), lambda i, k: (jnp.where(first(i, k), ai(i, k), n1 - 1),
                                                        jnp.where(first(i, k), ak(i, k), nk - 1))),
                   pl.BlockSpec((tm, tk), lambda i, k: (jnp.where(first(i, k), 0, ai(i, k) - n1),
                                                        jnp.where(first(i, k), 0, ak(i, k))))]
    in_specs = [*a_specs,
                pl.BlockSpec((None, tk, D), lambda i, k: (wl, ak(i, k), 0)),
                xspec, mspec(l, gate_idx), gspec(l)]
    args = [*a_parts, w, x, mod, gpost]
    out_specs = [xspec]
    out_shape = [jax.ShapeDtypeStruct((M, D), _F32)]
    if nxt is not None:
        gnext, gl, ml, sh_idx, sc_idx = nxt
        in_specs += [gspec(gl), mspec(ml, sh_idx), mspec(ml, sc_idx)]
        args += [gnext, mod, mod]
        out_specs.append(xspec)
        out_shape.append(jax.ShapeDtypeStruct((M, D), _BF))
    out = pl.pallas_call(
        functools.partial(_epi_kernel, n_a=len(a_parts), split=n1, nk=nk, te=te,
                          with_next=nxt is not None, rows=cfg.epi_rows),
        grid=(ni, nk + ne),
        in_specs=in_specs,
        out_specs=out_specs,
        out_shape=out_shape,
        scratch_shapes=[pltpu.VMEM((tm, D), _F32)],
        compiler_params=_params(("arbitrary", "arbitrary"), 56),
        name=name,
    )(*args)
    return out if nxt is not None else (out[0], None)


def _rg_kernel(*refs, S, Tc, nchunk, RB, want_final):
    (xb_ref, gt_ref, cw_ref, cb_ref, wa_ref, ba_ref, wx_ref, bx_ref, lam_ref, h0_ref), rest = refs[:10], refs[10:]
    if want_final:
        yg_ref, hfin_ref, xp_s, a_s, u_s, cin_s = rest
    else:
        yg_ref, xp_s, a_s, u_s, cin_s = rest
    R = S * Tc
    T = Tc * nchunk
    C = xb_ref.shape[-1]
    nlt = C // _LANES
    pad = _SUBLANES
    lanes = [slice(n * _LANES, (n + 1) * _LANES) for n in range(nlt)]

    nseq = R // T
    span = T + 2 * pad

    def fill(q, carry):
        base = pl.multiple_of(q * span, pad)
        xp_s[pl.ds(base, pad), :] = jnp.zeros((pad, C), _F32)
        xp_s[pl.ds(base + pad, T), :] = xb_ref[pl.ds(pl.multiple_of(q * T, T), T), :]
        xp_s[pl.ds(base + pad + T, pad), :] = jnp.zeros((pad, C), _F32)
        return carry

    lax.fori_loop(0, nseq, fill, 0)

    wa = (0.5 * wa_ref[...]).astype(_BF)
    wx = (0.5 * wx_ref[...]).astype(_BF)
    ba = 0.5 * ba_ref[...]
    bx = 0.5 * bx_ref[...]
    nceh = (0.5 * _RG_C) * jax.nn.softplus(-lam_ref[...])
    cw = cw_ref[...]
    cb = cb_ref[...]

    def gates(c, carry):
        r0 = pl.multiple_of(c * RB, RB)
        nwin = RB + 2 * pad
        s = r0 // Tc
        t0 = r0 % Tc
        dst = pl.ds(t0 * S + s, RB, stride=S)
        w0 = pl.multiple_of(r0 + (r0 // T) * 2 * pad, pad)
        for n, ln in enumerate(lanes):
            win = xp_s[pl.ds(w0, nwin), ln]
            x0 = win[pad:pad + RB]
            xm1 = pltpu.roll(win, 1, 0)[pad:pad + RB]
            xp1 = pltpu.roll(win, nwin - 1, 0)[pad:pad + RB]
            xp2 = pltpu.roll(win, nwin - 2, 0)[pad:pad + RB]
            xc = cb[:, ln] + xm1 * cw[0:1, ln] + x0 * cw[1:2, ln] + xp1 * cw[2:3, ln] + xp2 * cw[3:4, ln]
            xh = 0.5 * xc
            x16 = xc.astype(_BF)
            for e in range(2):
                tr = jnp.tanh(jnp.dot(x16, wa[e, n], preferred_element_type=_F32) + ba[e][:, ln])
                ti = jnp.tanh(jnp.dot(x16, wx[e, n], preferred_element_type=_F32) + bx[e][:, ln])
                nla = nceh[e][:, ln] * tr + nceh[e][:, ln]
                a = jnp.exp2(nla * (-_LOG2E))
                a_s[e, n, dst, :] = a
                w = jnp.tanh(nla) * (a * a + 1.0)
                sqrt_w = jnp.where(w > 0.0, w * lax.rsqrt(w), 0.0)
                u_s[e, n, dst, :] = sqrt_w * (ti * xh + xh)
        return carry

    lax.fori_loop(0, R // RB, gates, 0)

    chunked = nchunk > 1

    chains = [(e, n) for e in range(2) for n in range(nlt)]
    steps = _SCAN_STEPS

    def scan_block(tb, carry):
        h, p = list(carry[0]), list(carry[1])
        rows = [(pl.ds(pl.multiple_of((tb * steps + k) * S, S), S),
                 pl.ds(pl.multiple_of((Tc - 1 - tb * steps - k) * S, S), S)) for k in range(steps)]
        av = [[a_s[e, n, rows[k][e], :] for k in range(steps)] for e, n in chains]
        uv = [[u_s[e, n, rows[k][e], :] for k in range(steps)] for e, n in chains]
        hs, ps = [], []
        for k in range(steps):
            for i in range(len(chains)):
                h[i] = av[i][k] * h[i] + uv[i][k]
                hs.append(h[i])
                if chunked:
                    p[i] = av[i][k] * p[i]
                    ps.append(p[i])
        for k in range(steps):
            for i, (e, n) in enumerate(chains):
                u_s[e, n, rows[k][e], :] = hs[k * len(chains) + i]
                if chunked:
                    a_s[e, n, rows[k][e], :] = ps[k * len(chains) + i]
        return tuple(h), tuple(p)

    zeros = jnp.zeros((S, _LANES), _F32)
    ones = jnp.ones((S, _LANES), _F32)
    if chunked:
        init = ((zeros,) * (2 * nlt), (ones,) * (2 * nlt))
    else:
        init = (tuple(h0_ref[e][:, ln] for e in range(2) for ln in lanes), ())
    h, p = lax.fori_loop(0, Tc // steps, scan_block, init)

    if want_final:
        for e in range(2):
            for n, ln in enumerate(lanes):
                hfin_ref[e, :, ln] = h[e * nlt + n]

    if chunked:
        chunk_id = lax.broadcasted_iota(jnp.int32, (S, _LANES), 0) % nchunk
        edge = (chunk_id == 0, chunk_id == nchunk - 1)
        shift = (1, S - 1)
        for e in range(2):
            for n, ln in enumerate(lanes):
                h0 = h0_ref[e][:, ln]
                cin = jnp.where(edge[e], h0, 0.0)
                for _ in range(nchunk - 1):
                    cin = jnp.where(edge[e], h0, pltpu.roll(h[e * nlt + n] + p[e * nlt + n] * cin, shift[e], 0))
                cin_s[e, n] = cin

    def emit(c, carry):
        r0 = pl.multiple_of(c * RB, RB)
        s = r0 // Tc
        t0 = r0 % Tc
        src = pl.ds(t0 * S + s, RB, stride=S)
        r = pl.ds(r0, RB)
        for n, ln in enumerate(lanes):
            y = []
            for e in range(2):
                ye = u_s[e, n, src, :]
                if chunked:
                    ye = ye + a_s[e, n, src, :] * cin_s[e, n, pl.ds(s, 1), :]
                y.append(ye)
            yg_ref[r, ln] = ((y[0] + y[1]) * jax.nn.gelu(gt_ref[r, ln])).astype(_BF)
        return carry

    lax.fori_loop(0, R // RB, emit, 0)


def _rg_core(cfg, proj, row_off, S, Tc, nchunk, h0, rgw, j, want_final):
    conv_w, conv_b, w_a, b_a, w_x, b_x, lam = rgw
    rnn = proj.shape[1] // 2
    C = cfg.rg_lanes
    blk = rnn // cfg.rg_blocks
    assert blk == _LANES, "one gate block per lane tile"
    nct = rnn // C
    R = S * Tc
    RB = min(cfg.rg_rows, Tc)
    rb0 = row_off // R
    nrg = conv_w.shape[0]
    vec = lambda arr: arr.reshape(nrg, 2, 1, rnn)
    vspec = pl.BlockSpec((None, 2, 1, C), lambda ct: (j, 0, 0, ct))
    wspec = pl.BlockSpec((None, 2, C // blk, blk, blk), lambda ct: (j, 0, ct, 0, 0))
    out_specs = [pl.BlockSpec((R, C), lambda ct: (0, ct))]
    out_shape = [jax.ShapeDtypeStruct((R, rnn), _BF)]
    if want_final:
        out_specs.append(pl.BlockSpec((2, S, C), lambda ct: (0, 0, ct)))
        out_shape.append(jax.ShapeDtypeStruct((2, S, rnn), _F32))
    return pl.pallas_call(
        functools.partial(_rg_kernel, S=S, Tc=Tc, nchunk=nchunk, RB=RB, want_final=want_final),
        grid=(nct,),
        in_specs=[pl.BlockSpec((R, C), lambda ct: (rb0, nct + ct)),
                  pl.BlockSpec((R, C), lambda ct: (rb0, ct)),
                  pl.BlockSpec((None, _CONV_W, C), lambda ct: (j, 0, ct)),
                  pl.BlockSpec((None, 1, C), lambda ct: (j, 0, ct)),
                  wspec, vspec, wspec, vspec, vspec,
                  pl.BlockSpec((2, S, C), lambda ct: (0, 0, ct))],
        out_specs=out_specs,
        out_shape=out_shape,
        scratch_shapes=[pltpu.VMEM((R + (R // (Tc * nchunk)) * 2 * _SUBLANES, C), _F32),
                        pltpu.VMEM((2, C // _LANES, R, _LANES), _F32),
                        pltpu.VMEM((2, C // _LANES, R, _LANES), _F32),
                        pltpu.VMEM((2, C // _LANES, S, _LANES), _F32)],
        compiler_params=_params(("arbitrary",), 56),
        name="rg_core_ctx" if want_final else "rg_core_lat",
    )(proj, proj, conv_w, conv_b.reshape(nrg, 1, rnn), w_a, vec(b_a), w_x, vec(b_x), vec(lam), h0)


def _attn_kernel(*refs, G, hd, scale, has_cache):
    if has_cache:
        q_ref, k_ref, v_ref, ck_ref, cv_ref, o_ref, k_s, v_s, kc_s, vc_s = refs
    else:
        q_ref, k_ref, v_ref, o_ref, k_s, v_s = refs

    @pl.when(pl.program_id(2) == 0)
    def _cast_keys():
        k_s[...] = k_ref[...].astype(_BF)
        v_s[...] = v_ref[...].astype(_BF)
        if has_cache:
            kc_s[...] = ck_ref[...].astype(_BF)
            vc_s[...] = cv_ref[...].astype(_BF)

    nt = (((1,), (1,)), ((), ()))
    for kvh in range(k_s.shape[-1] // hd):
        cols = slice(kvh * hd, (kvh + 1) * hd)
        k = k_s[:, cols]
        v = v_s[:, cols]
        if has_cache:
            kc = kc_s[:, cols]
            vc = vc_s[:, cols]
        for g in range(G):
            qc = slice((kvh * G + g) * hd, (kvh * G + g + 1) * hd)
            q = (q_ref[:, qc] * scale).astype(_BF)
            s1 = lax.dot_general(q, k, nt, preferred_element_type=_F32)
            m = jnp.max(s1, axis=-1, keepdims=True)
            if has_cache:
                s2 = lax.dot_general(q, kc, nt, preferred_element_type=_F32)
                m = jnp.maximum(m, jnp.max(s2, axis=-1, keepdims=True))
            p1 = jnp.exp(s1 - m)
            l = jnp.sum(p1, axis=-1, keepdims=True)
            o = jnp.dot(p1.astype(_BF), v, preferred_element_type=_F32)
            if has_cache:
                p2 = jnp.exp(s2 - m)
                l = l + jnp.sum(p2, axis=-1, keepdims=True)
                o = o + jnp.dot(p2.astype(_BF), vc, preferred_element_type=_F32)
            o_ref[:, qc] = (o / l).astype(_BF)


def _attention(cfg, qkv, row_off, B, T, tq, cache, j, name, kvs=1):
    hd = cfg.d_model // cfg.n_heads
    KV = cfg.n_kv_heads
    G = cfg.n_heads // KV
    nq = T // tq
    qb0 = row_off // tq
    kb0 = row_off // T
    kcol = cfg.n_heads // kvs
    vcol = (cfg.n_heads + KV) // kvs
    in_specs = [pl.BlockSpec((tq, kvs * G * hd), lambda b, h, qi: (qb0 + b * nq + qi, h)),
                pl.BlockSpec((T, kvs * hd), lambda b, h, qi: (kb0 + b, kcol + h)),
                pl.BlockSpec((T, kvs * hd), lambda b, h, qi: (kb0 + b, vcol + h))]
    args = [qkv, qkv, qkv]
    scratch = [pltpu.VMEM((T, kvs * hd), _BF), pltpu.VMEM((T, kvs * hd), _BF)]
    if cache is not None:
        ck, cv = cache
        P = ck.shape[2]
        cspec = pl.BlockSpec((None, None, P, kvs * hd), lambda b, h, qi: (b, j, 0, h))
        in_specs += [cspec, cspec]
        args += [ck, cv]
        scratch += [pltpu.VMEM((P, kvs * hd), _BF), pltpu.VMEM((P, kvs * hd), _BF)]
    return pl.pallas_call(
        functools.partial(_attn_kernel, G=G, hd=hd, scale=1.0 / math.sqrt(hd), has_cache=cache is not None),
        grid=(B, KV // kvs, nq),
        in_specs=in_specs,
        out_specs=pl.BlockSpec((tq, kvs * G * hd), lambda b, h, qi: (b * nq + qi, h)),
        out_shape=jax.ShapeDtypeStruct((B * T, cfg.n_heads * hd), _BF),
        scratch_shapes=scratch,
        compiler_params=_params(("arbitrary", "arbitrary", "arbitrary"), 58),
        name=name,
    )(*args)


def _forward(cfg, x_prompt, x_sample, state_rglru, cache_k, cache_v, c, c_ctx,
             w_mod, b_mod, g_pre_mix, g_post_mix, g_pre_ffn, g_post_ffn,
             w_qkv, g_q, g_k, w_o,
             w_rg_in, rg_conv_w, rg_conv_b, w_rg_a, b_rg_a, w_rg_x, b_rg_x, rg_lambda, w_rg_out,
             w_ff1, w_ff2):
    D = cfg.d_model
    L = cfg.depth
    B, T = cfg.batch, cfg.seq
    DB, DT = cfg.dec_batch, cfg.dec_seq
    NP = B * T
    hd = D // cfg.n_heads
    KV = cfg.n_kv_heads
    nch = cfg.dec_chunks

    cvec = jnp.zeros((_SUBLANES, D), _F32).at[0].set(c_ctx).at[1:1 + DB].set(c)
    mod = _modulation(cfg, cvec, w_mod, b_mod)[:, :1 + DB].reshape(L, 1 + DB, 6, 1, D)

    gains = lambda g: g.reshape(g.shape[0], 1, g.shape[1])
    g_pre_mix, g_post_mix, g_pre_ffn, g_post_ffn = map(gains, (g_pre_mix, g_post_mix, g_pre_ffn, g_post_ffn))
    g_q, g_k = gains(g_q), gains(g_k)
    rope = _rope_tables(cfg)
    ck = cache_k.reshape(DB, -1, cfg.past_len, KV * hd)
    cv = cache_v.reshape(DB, -1, cfg.past_len, KV * hd)

    x, h = _prenorm(cfg, x_prompt.reshape(NP, D), x_sample.reshape(DB * DT, D), g_pre_mix, mod, 0)
    new_states, new_ks, new_vs = [], [], []
    for l in range(L):
        j = l // 2
        if l % 2 == 0:
            proj = _wres_call(_inproj_kernel, cfg, h, w_rg_in, j, cfg.tn_in, _F32, "rg_in_proj")
            rgw = (rg_conv_w, rg_conv_b, w_rg_a, b_rg_a, w_rg_x, b_rg_x, rg_lambda)
            h0p = jnp.zeros((2, B, D), _F32)
            yp, st = _rg_core(cfg, proj, 0, B, T, 1, h0p, rgw, j, True)
            h0s = jnp.repeat(jnp.swapaxes(state_rglru[:, j], 0, 1), nch, axis=1)
            (ys,) = _rg_core(cfg, proj, NP, DB * nch, DT // nch, nch, h0s, rgw, j, False)
            mix = (yp, ys)
            new_states.append(jnp.swapaxes(st, 0, 1))
            w_out, wl = w_rg_out, j
        else:
            qkv = _qkv(cfg, h, w_qkv, g_q, g_k, rope, j)
            kcols = slice(cfg.n_heads * hd, (cfg.n_heads + KV) * hd)
            vcols = slice((cfg.n_heads + KV) * hd, (cfg.n_heads + 2 * KV) * hd)
            new_ks.append(qkv[:NP, kcols].reshape(B, T, KV, hd))
            new_vs.append(qkv[:NP, vcols].reshape(B, T, KV, hd))
            op = _attention(cfg, qkv, 0, B, T, T, None, j, "attn_ctx", kvs=KV)
            os_ = _attention(cfg, qkv, NP, DB, DT, cfg.tq, (ck, cv), j, "attn_lat")
            mix = (op, os_)
            w_out, wl = w_o, j
        x, f = _epi(cfg, mix, w_out, wl, x, mod, l, 2, g_post_mix,
                    (g_pre_ffn, l, l, 3, 4), "mix_out_proj")
        hdn = _wres_call(_ffn1_kernel, cfg, f, w_ff1, l, cfg.tn_ff, _BF, "ffn_up", tm=cfg.tm_ff)
        nxt = (g_pre_mix, l + 1, l + 1, 0, 1) if l + 1 < L else None
        x, h = _epi(cfg, (hdn,), w_ff2, l, x, mod, l, 5, g_post_ffn, nxt, "ffn_down")

    return (x[:NP].reshape(B, T, D), x[NP:].reshape(DB, DT, D),
            jnp.stack(new_states, axis=1), jnp.stack(new_ks, axis=1), jnp.stack(new_vs, axis=1))


def kernel(x_prompt, x_sample, state_rglru, cache_k, cache_v, c, c_ctx, w_mod, b_mod, g_pre_mix, g_post_mix, g_pre_ffn, g_post_ffn, w_qkv, g_q, g_k, w_o, w_rg_in, rg_conv_w, rg_conv_b, w_rg_a, b_rg_a, w_rg_x, b_rg_x, rg_lambda, w_rg_out, w_ff1, w_ff2):
    return _forward(_CFG, x_prompt, x_sample, state_rglru, cache_k, cache_v, c, c_ctx,
                    w_mod, b_mod, g_pre_mix, g_post_mix, g_pre_ffn, g_post_ffn,
                    w_qkv, g_q, g_k, w_o,
                    w_rg_in, rg_conv_w, rg_conv_b, w_rg_a, b_rg_a, w_rg_x, b_rg_x, rg_lambda, w_rg_out,
                    w_ff1, w_ff2)
```
